```python
import math
import jax
import jax.numpy as jnp
from jax import lax
import numpy as np

D_MODEL = 2048
BATCH = 1
SEQ = 8192
DEPTH = 4

CHUNK = 64
Q_BLOCK = 128
N_MIXERS = 3
EPS = 1e-6
D_FF = ((8 * D_MODEL) // 3 + 255) // 256 * 256

A_HEADS = D_MODEL // 256
A_HEAD_DIM = 128
A_V_DIM = 2 * A_HEAD_DIM
A_QK = 2 * A_HEADS * A_HEAD_DIM
A_OUT = A_HEADS * A_V_DIM
A_IN = 2 * A_QK + A_OUT

B_HEADS = D_MODEL // 128
B_HEAD_DIM = 128
B_W = B_HEADS * B_HEAD_DIM
B_IN = 4 * B_W + B_HEADS

C_QK_HEADS = D_MODEL // 128
C_V_HEADS = 2 * C_QK_HEADS
C_HEAD_DIM = 128
C_QK = C_QK_HEADS * C_HEAD_DIM
C_VW = C_V_HEADS * C_HEAD_DIM
C_CONV = 4
C_IN = 2 * C_QK + 2 * C_VW + 2 * C_V_HEADS

N_A = (DEPTH + 2) // 3
N_B = (DEPTH + 1) // 3
N_C = DEPTH // 3

kernel_name = 'hybrid_diff_fox_gdn_trunk'


def rms_norm(x, w):
    xf = x.astype(jnp.float32)
    y = xf * lax.rsqrt(jnp.mean(xf * xf, axis=-1, keepdims=True) + EPS)
    return (y * w.astype(jnp.float32)).astype(x.dtype)


def l2_normalize(x):
    xf = x.astype(jnp.float32)
    return xf * lax.rsqrt(jnp.sum(xf * xf, axis=-1, keepdims=True) + EPS)


def to_query_blocks(t):
    b, s = t.shape[:2]
    t = t.reshape((b, s // Q_BLOCK, Q_BLOCK) + t.shape[2:])
    return jnp.moveaxis(t, 1, 0)


def from_query_blocks(t):
    t = jnp.moveaxis(t, 0, 1)
    return t.reshape((t.shape[0], -1) + t.shape[3:])


def diff_lambda_init(layer_idx):
    return 0.8 - 0.6 * math.exp(-0.3 * layer_idx)


def differential_attention(xn, w_in, w_out, lam_q1, lam_k1, lam_q2, lam_k2, sub_norm, lambda_init):
    b, s, _ = xn.shape
    q, k, v = jnp.split(xn @ w_in, [A_QK, 2 * A_QK], axis=-1)
    q = q.reshape(b, s, 2, A_HEADS, A_HEAD_DIM)
    k = k.reshape(b, s, 2, A_HEADS, A_HEAD_DIM)
    v = v.reshape(b, s, A_HEADS, A_V_DIM)
    f32 = jnp.float32
    lam = (jnp.exp(jnp.sum((lam_q1 * lam_k1).astype(f32)))
           - jnp.exp(jnp.sum((lam_q2 * lam_k2).astype(f32))) + lambda_init)
    scale = A_HEAD_DIM ** -0.5
    k_chunk = jnp.arange(s) // CHUNK

    def block(args):
        qb, qpos = args
        sc = jnp.einsum('bqmhd,bkmhd->bmhqk', qb, k, preferred_element_type=f32) * scale
        visible = k_chunk[None, :] <= (qpos // CHUNK)[:, None]
        p = jax.nn.softmax(jnp.where(visible, sc, -jnp.inf), axis=-1)
        a = p[:, 0] - lam * p[:, 1]
        return jnp.einsum('bhqk,bkhe->bqhe', a.astype(v.dtype), v)

    qpos = jnp.arange(s).reshape(-1, Q_BLOCK)
    o = from_query_blocks(lax.map(block, (to_query_blocks(q), qpos)))
    o = rms_norm(o, sub_norm) * (1.0 - lambda_init)
    return o.reshape(b, s, A_OUT) @ w_out


def forgetting_attention(xn, w_in, w_out, forget_bias, q_norm, k_norm):
    b, s, _ = xn.shape
    f32 = jnp.float32
    q, k, v, gate, f_logit = jnp.split(xn @ w_in, [B_W, 2 * B_W, 3 * B_W, 4 * B_W], axis=-1)
    q = rms_norm(q.reshape(b, s, B_HEADS, B_HEAD_DIM), q_norm)
    k = rms_norm(k.reshape(b, s, B_HEADS, B_HEAD_DIM), k_norm)
    v = v.reshape(b, s, B_HEADS, B_HEAD_DIM)
    log_f = jax.nn.log_sigmoid((f_logit + forget_bias).astype(f32))
    cum = jnp.cumsum(log_f, axis=1)
    cum_k = jnp.transpose(cum, (0, 2, 1))
    scale = B_HEAD_DIM ** -0.5
    kpos = jnp.arange(s)

    def block(args):
        qb, cq, qpos = args
        sc = jnp.einsum('bqhd,bkhd->bhqk', qb, k, preferred_element_type=f32) * scale
        sc = sc + jnp.transpose(cq, (0, 2, 1))[..., :, None] - cum_k[..., None, :]
        visible = kpos[None, :] <= qpos[:, None]
        p = jax.nn.softmax(jnp.where(visible, sc, -jnp.inf), axis=-1)
        return jnp.einsum('bhqk,bkhd->bqhd', p.astype(v.dtype), v)

    qpos = jnp.arange(s).reshape(-1, Q_BLOCK)
    o = from_query_blocks(lax.map(block, (to_query_blocks(q), to_query_blocks(cum), qpos)))
    o = o.reshape(b, s, B_W) * jax.nn.sigmoid(gate)
    return o @ w_out


def causal_depthwise_conv(x, w):
    kw = w.shape[0]
    s = x.shape[1]
    xp = jnp.pad(x, ((0, 0), (kw - 1, 0), (0, 0)))
    return sum(xp[:, j:j + s, :] * w[j] for j in range(kw))


def chunked_gated_delta_rule(q, k, v, g, beta):
    b, s, h, dk = q.shape
    dv = v.shape[-1]
    n = s // CHUNK
    out_dtype = v.dtype
    f32 = jnp.float32

    def chunks(t):
        t = jnp.moveaxis(t.astype(f32), 2, 1)
        return t.reshape((b, h, n, CHUNK) + t.shape[3:])

    q, k, v, g, beta = chunks(q), chunks(k), chunks(v), chunks(g), chunks(beta)
    gc = jnp.cumsum(g, axis=-1)
    causal = jnp.tril(jnp.ones((CHUNK, CHUNK), dtype=bool))
    strict = jnp.tril(jnp.ones((CHUNK, CHUNK), dtype=bool), -1)
    decay = jnp.exp(jnp.where(causal, gc[..., :, None] - gc[..., None, :], -jnp.inf))
    kb = k * beta[..., None]
    lower = jnp.where(strict, jnp.einsum('bhnid,bhnjd->bhnij', kb, k) * decay, 0.0)
    tmat = lower + jnp.eye(CHUNK, dtype=f32)
    rhs = jnp.concatenate([v * beta[..., None], kb * jnp.exp(gc)[..., None]], axis=-1)
    sol = lax.linalg.triangular_solve(tmat, rhs, left_side=True, lower=True, unit_diagonal=True)
    u, w = sol[..., :dv], sol[..., dv:]
    intra = jnp.where(causal, jnp.einsum('bhnid,bhnjd->bhnij', q, k) * decay, 0.0)
    g_last = gc[..., -1]
    q_dec = q * jnp.exp(gc)[..., None]
    k_dec = k * jnp.exp(g_last[..., None] - gc)[..., None]
    xs = (jnp.moveaxis(q_dec, 2, 0), jnp.moveaxis(k_dec, 2, 0), jnp.moveaxis(u, 2, 0),
          jnp.moveaxis(w, 2, 0), jnp.moveaxis(intra, 2, 0), jnp.moveaxis(g_last, 2, 0))

    def step(state, inp):
        qd, kd, ui, wi, ai, gl = inp
        v_new = ui - jnp.einsum('bhcd,bhde->bhce', wi, state)
        o = jnp.einsum('bhcd,bhde->bhce', qd, state) + jnp.einsum('bhij,bhje->bhie', ai, v_new)
        state = state * jnp.exp(gl)[..., None, None] + jnp.einsum('bhcd,bhce->bhde', kd, v_new)
        return state, o

    state0 = jnp.zeros((b, h, dk, dv), dtype=f32)
    _, o = lax.scan(step, state0, xs)
    o = jnp.moveaxis(o, 0, 2).reshape(b, h, s, dv)
    return jnp.moveaxis(o, 1, 2).astype(out_dtype)


def gated_deltanet(xn, w_in, w_out, conv_w, a_log, dt_bias, out_norm):
    b, s, _ = xn.shape
    f32 = jnp.float32
    qkv, z, beta_logit, a = jnp.split(
        xn @ w_in, [2 * C_QK + C_VW, 2 * C_QK + 2 * C_VW, 2 * C_QK + 2 * C_VW + C_V_HEADS], axis=-1)
    qkv = jax.nn.silu(causal_depthwise_conv(qkv, conv_w))
    q, k, v = jnp.split(qkv, [C_QK, 2 * C_QK], axis=-1)
    rep = C_V_HEADS // C_QK_HEADS
    q = jnp.repeat(l2_normalize(q.reshape(b, s, C_QK_HEADS, C_HEAD_DIM)), rep, axis=2) * (C_HEAD_DIM ** -0.5)
    k = jnp.repeat(l2_normalize(k.reshape(b, s, C_QK_HEADS, C_HEAD_DIM)), rep, axis=2)
    v = v.reshape(b, s, C_V_HEADS, C_HEAD_DIM)
    beta = jax.nn.sigmoid(beta_logit.astype(f32))
    g = -jnp.exp(a_log.astype(f32)) * jax.nn.softplus((a + dt_bias).astype(f32))
    o = chunked_gated_delta_rule(q, k, v, g, beta)
    o = rms_norm(o, out_norm) * jax.nn.silu(z.reshape(b, s, C_V_HEADS, C_HEAD_DIM))
    return o.reshape(b, s, C_VW) @ w_out


def swiglu(xn, w_gate, w_up, w_down):
    return (jax.nn.silu(xn @ w_gate) * (xn @ w_up)) @ w_down


def setup_inputs(seed: int = 0) -> dict:
    key = jax.random.key(seed)
    ks = jax.random.split(key, 32)
    f32 = jnp.float32

    def nrm(i, shape, scale):
        return jax.random.normal(ks[i], shape, f32) * scale

    def gain(i, shape):
        return 1.0 + 0.05 * jax.random.normal(ks[i], shape, f32)

    dt = jnp.exp(jax.random.uniform(ks[25], (N_C, C_V_HEADS), f32, math.log(1e-3), math.log(1e-1)))
    dt_bias = dt + jnp.log(-jnp.expm1(-dt))
    a_log = jnp.log(jax.random.uniform(ks[26], (N_C, C_V_HEADS), f32, 1.0, 16.0))
    forget_bias = jax.random.uniform(ks[27], (N_B, B_HEADS), f32, 1.0, 4.0)
    return {
        'x': nrm(0, (BATCH, SEQ, D_MODEL), 1.0),
        'mix_norm': gain(1, (DEPTH, D_MODEL)),
        'ffn_norm': gain(2, (DEPTH, D_MODEL)),
        'final_norm': gain(3, (D_MODEL,)),
        'a_w_in': nrm(4, (N_A, D_MODEL, A_IN), D_MODEL ** -0.5),
        'a_w_out': nrm(5, (N_A, A_OUT, D_MODEL), A_OUT ** -0.5),
        'a_lam_q1': nrm(6, (N_A, A_HEAD_DIM), 0.1),
        'a_lam_k1': nrm(7, (N_A, A_HEAD_DIM), 0.1),
        'a_lam_q2': nrm(8, (N_A, A_HEAD_DIM), 0.1),
        'a_lam_k2': nrm(9, (N_A, A_HEAD_DIM), 0.1),
        'a_sub_norm': gain(10, (N_A, A_V_DIM)),
        'b_w_in': nrm(11, (N_B, D_MODEL, B_IN), D_MODEL ** -0.5),
        'b_w_out': nrm(12, (N_B, B_W, D_MODEL), B_W ** -0.5),
        'b_forget_bias': forget_bias,
        'b_q_norm': gain(13, (N_B, B_HEAD_DIM)),
        'b_k_norm': gain(14, (N_B, B_HEAD_DIM)),
        'c_w_in': nrm(15, (N_C, D_MODEL, C_IN), D_MODEL ** -0.5),
        'c_w_out': nrm(16, (N_C, C_VW, D_MODEL), C_VW ** -0.5),
        'c_conv_w': nrm(17, (N_C, C_CONV, 2 * C_QK + C_VW), C_CONV ** -0.5),
        'c_a_log': a_log,
        'c_dt_bias': dt_bias,
        'c_out_norm': gain(18, (N_C, C_HEAD_DIM)),
        'ffn_w_gate': nrm(19, (DEPTH, D_MODEL, D_FF), D_MODEL ** -0.5),
        'ffn_w_up': nrm(20, (DEPTH, D_MODEL, D_FF), D_MODEL ** -0.5),
        'ffn_w_down': nrm(21, (DEPTH, D_FF, D_MODEL), D_FF ** -0.5),
    }


def reference(x, mix_norm, ffn_norm, final_norm, a_w_in, a_w_out, a_lam_q1, a_lam_k1, a_lam_q2, a_lam_k2,
              a_sub_norm, b_w_in, b_w_out, b_forget_bias, b_q_norm, b_k_norm, c_w_in, c_w_out, c_conv_w,
              c_a_log, c_dt_bias, c_out_norm, ffn_w_gate, ffn_w_up, ffn_w_down):
    h = x
    for i in range(DEPTH):
        slot = i // N_MIXERS
        xn = rms_norm(h, mix_norm[i])
        if i % N_MIXERS == 0:
            mixed = differential_attention(xn, a_w_in[slot], a_w_out[slot], a_lam_q1[slot], a_lam_k1[slot],
                                           a_lam_q2[slot], a_lam_k2[slot], a_sub_norm[slot], diff_lambda_init(i))
        elif i % N_MIXERS == 1:
            mixed = forgetting_attention(xn, b_w_in[slot], b_w_out[slot], b_forget_bias[slot],
                                         b_q_norm[slot], b_k_norm[slot])
        else:
            mixed = gated_deltanet(xn, c_w_in[slot], c_w_out[slot], c_conv_w[slot], c_a_log[slot],
                                   c_dt_bias[slot], c_out_norm[slot])
        h = h + mixed.astype(h.dtype)
        h = h + swiglu(rms_norm(h, ffn_norm[i]), ffn_w_gate[i], ffn_w_up[i], ffn_w_down[i]).astype(h.dtype)
    return rms_norm(h, final_norm)
```

```python
import functools
import math

import jax
import jax.numpy as jnp
from jax import lax
from jax.experimental import pallas as pl
from jax.experimental.pallas import tpu as pltpu

F32 = jnp.float32
BF16 = jnp.bfloat16
I32 = jnp.int32

EPS = 1e-6
MASK_CHUNK = 64
HEAD = 128
LANES = 128
N_MIXERS = 3
DELTA_CHUNK = 128
VMEM_LIMIT_BYTES = 52 * 1024 * 1024


def _cparams(*sem):
    return pltpu.CompilerParams(dimension_semantics=sem, vmem_limit_bytes=VMEM_LIMIT_BYTES)


def _dot(a, b):
    return jnp.dot(a, b, preferred_element_type=F32)


def _dot_nt(a, b):
    return lax.dot_general(a, b, (((1,), (1,)), ((), ())), preferred_element_type=F32)


def _rms_rows(x):
    return x * lax.rsqrt(jnp.mean(x * x, axis=-1, keepdims=True) + EPS)


def _proj_kernel(*refs, tn, norm_blocks, has_colscale):
    if has_colscale:
        x_ref, g_ref, w_ref, cs_ref, o_ref, xn_ref = refs
    else:
        x_ref, g_ref, w_ref, o_ref, xn_ref = refs
        cs_ref = None
    n = pl.program_id(1)

    @pl.when(n == 0)
    def _():
        xn_ref[...] = (_rms_rows(x_ref[...]) * g_ref[...]).astype(BF16)

    acc = _dot(xn_ref[...], w_ref[...])

    def finish(a):
        if cs_ref is not None:
            a = a * cs_ref[...]
        o_ref[...] = a.astype(o_ref.dtype)

    if norm_blocks == 0:
        finish(acc)
    else:
        @pl.when(n < norm_blocks)
        def _():
            parts = [_rms_rows(acc[:, c * HEAD:(c + 1) * HEAD]) for c in range(tn // HEAD)]
            finish(jnp.concatenate(parts, axis=1))

        @pl.when(n >= norm_blocks)
        def _():
            finish(acc)


def _norm_proj(h, gain, w, colscale=None, norm_cols=0, out_dtype=BF16, tm=512, tn=512):
    s, d = h.shape
    n = w.shape[1]
    tn = min(tn, n)
    assert s % tm == 0 and n % tn == 0 and norm_cols % tn == 0 and tn % HEAD == 0
    in_specs = [
        pl.BlockSpec((tm, d), lambda m, j: (m, 0)),
        pl.BlockSpec((1, d), lambda m, j: (0, 0)),
        pl.BlockSpec((d, tn), lambda m, j: (0, j)),
    ]
    args = [h, gain.reshape(1, d), w]
    if colscale is not None:
        in_specs.append(pl.BlockSpec((1, tn), lambda m, j: (0, j)))
        args.append(colscale.reshape(1, n))
    return pl.pallas_call(
        functools.partial(_proj_kernel, tn=tn, norm_blocks=norm_cols // tn,
                          has_colscale=colscale is not None),
        grid=(s // tm, n // tn),
        in_specs=in_specs,
        out_specs=pl.BlockSpec((tm, tn), lambda m, j: (m, j)),
        out_shape=jax.ShapeDtypeStruct((s, n), out_dtype),
        scratch_shapes=[pltpu.VMEM((tm, d), BF16)],
        compiler_params=_cparams("parallel", "arbitrary"),
        name="norm_proj",
    )(*args)


def _out_proj_kernel(o_ref, w_ref, h_ref, y_ref):
    y_ref[...] = h_ref[...] + _dot(o_ref[...], w_ref[...])


def _out_proj(o, w, h, tm=512, tn=512):
    s, k = o.shape
    d = w.shape[1]
    return pl.pallas_call(
        _out_proj_kernel,
        grid=(s // tm, d // tn),
        in_specs=[
            pl.BlockSpec((tm, k), lambda m, j: (m, 0)),
            pl.BlockSpec((k, tn), lambda m, j: (0, j)),
            pl.BlockSpec((tm, tn), lambda m, j: (m, j)),
        ],
        out_specs=pl.BlockSpec((tm, tn), lambda m, j: (m, j)),
        out_shape=jax.ShapeDtypeStruct((s, d), F32),
        compiler_params=_cparams("parallel", "arbitrary"),
        name="out_proj",
    )(o, w, h)


def _ffn_kernel(*refs, has_final):
    if has_final:
        h_ref, g_ref, wg_ref, wu_ref, wd_ref, fg_ref, y_ref, xn_ref = refs
    else:
        h_ref, g_ref, wg_ref, wu_ref, wd_ref, y_ref, xn_ref = refs
        fg_ref = None
    f = pl.program_id(1)

    @pl.when(f == 0)
    def _():
        h = h_ref[...]
        xn_ref[...] = (_rms_rows(h) * g_ref[...]).astype(BF16)
        y_ref[...] = h

    xn = xn_ref[...]
    gate = _dot(xn, wg_ref[...])
    up = _dot(xn, wu_ref[...])
    act = (gate * jax.nn.sigmoid(gate) * up).astype(BF16)
    y_ref[...] += _dot(act, wd_ref[...])

    if fg_ref is not None:
        @pl.when(f == pl.num_programs(1) - 1)
        def _():
            y_ref[...] = _rms_rows(y_ref[...]) * fg_ref[...]


def _ffn(h, gain, w_gate, w_up, w_down, final_gain=None, tm=512, tf=512):
    s, d = h.shape
    dff = w_gate.shape[1]
    in_specs = [
        pl.BlockSpec((tm, d), lambda m, f: (m, 0)),
        pl.BlockSpec((1, d), lambda m, f: (0, 0)),
        pl.BlockSpec((d, tf), lambda m, f: (0, f)),
        pl.BlockSpec((d, tf), lambda m, f: (0, f)),
        pl.BlockSpec((tf, d), lambda m, f: (f, 0)),
    ]
    args = [h, gain.reshape(1, d), w_gate, w_up, w_down]
    if final_gain is not None:
        in_specs.append(pl.BlockSpec((1, d), lambda m, f: (0, 0)))
        args.append(final_gain.reshape(1, d))
    return pl.pallas_call(
        functools.partial(_ffn_kernel, has_final=final_gain is not None),
        grid=(s // tm, dff // tf),
        in_specs=in_specs,
        out_specs=pl.BlockSpec((tm, d), lambda m, f: (m, 0)),
        out_shape=jax.ShapeDtypeStruct((s, d), F32),
        scratch_shapes=[pltpu.VMEM((tm, d), BF16)],
        compiler_params=_cparams("parallel", "arbitrary"),
        name="swiglu",
    )(*args)


def _diff_attn_kernel(q1_ref, q2_ref, k1_ref, k2_ref, v_ref, lam_ref, sn_ref, o_ref,
                      m_ref, l_ref, acc_ref, *, tq, tk, lambda_init):
    q0 = pl.program_id(1) * tq
    m_ref[...] = jnp.full(m_ref.shape, -jnp.inf, F32)
    l_ref[...] = jnp.zeros(l_ref.shape, F32)
    acc_ref[...] = jnp.zeros(acc_ref.shape, F32)
    qs = (q1_ref[...], q2_ref[...])
    ks = (k1_ref, k2_ref)

    def step(j, masked):
        k0 = pl.multiple_of(j * tk, tk)
        v = v_ref[pl.ds(k0, tk), :]
        if masked:
            row = q0 + lax.broadcasted_iota(I32, (tq, tk), 0)
            col = k0 + lax.broadcasted_iota(I32, (tq, tk), 1)
            visible = (col // MASK_CHUNK) <= (row // MASK_CHUNK)
        for mp in range(2):
            s = _dot_nt(qs[mp], ks[mp][pl.ds(k0, tk), :])
            if masked:
                s = jnp.where(visible, s, -jnp.inf)
            m_prev = m_ref[mp]
            m_new = jnp.maximum(m_prev, jnp.max(s, axis=-1, keepdims=True))
            alpha = jnp.exp(m_prev - m_new)
            p = jnp.exp(s - m_new)
            l_ref[mp] = alpha * l_ref[mp] + jnp.sum(p, axis=-1, keepdims=True)
            acc_ref[mp] = alpha * acc_ref[mp] + _dot(p.astype(BF16), v)
            m_ref[mp] = m_new

    n_full = q0 // tk

    def body(j, carry):
        step(j, False)
        return carry

    lax.fori_loop(0, n_full, body, 0)
    step(n_full, True)

    lam = (jnp.exp(jnp.sum(lam_ref[0:1, :] * lam_ref[1:2, :], keepdims=True))
           - jnp.exp(jnp.sum(lam_ref[2:3, :] * lam_ref[3:4, :], keepdims=True)) + lambda_init)
    o = acc_ref[0] / l_ref[0] - lam * (acc_ref[1] / l_ref[1])
    o = _rms_rows(o) * sn_ref[...] * (1.0 - lambda_init)
    o_ref[...] = o.astype(o_ref.dtype)


def _diff_attention(qkv, lam_params, sub_norm, lambda_init, heads, tq=512):
    s = qkv.shape[0]
    vd = 2 * HEAD
    tk = tq
    hb = heads
    return pl.pallas_call(
        functools.partial(_diff_attn_kernel, tq=tq, tk=tk, lambda_init=lambda_init),
        grid=(heads, s // tq),
        in_specs=[
            pl.BlockSpec((tq, HEAD), lambda h, i: (i, h)),
            pl.BlockSpec((tq, HEAD), lambda h, i: (i, hb + h)),
            pl.BlockSpec((s, HEAD), lambda h, i: (0, 2 * hb + h)),
            pl.BlockSpec((s, HEAD), lambda h, i: (0, 3 * hb + h)),
            pl.BlockSpec((s, vd), lambda h, i: (0, 2 * hb + h)),
            pl.BlockSpec((4, HEAD), lambda h, i: (0, 0)),
            pl.BlockSpec((1, vd), lambda h, i: (0, 0)),
        ],
        out_specs=pl.BlockSpec((tq, vd), lambda h, i: (i, h)),
        out_shape=jax.ShapeDtypeStruct((s, heads * vd), BF16),
        scratch_shapes=[pltpu.VMEM((2, tq, 1), F32), pltpu.VMEM((2, tq, 1), F32),
                        pltpu.VMEM((2, tq, vd), F32)],
        compiler_params=_cparams("parallel", "arbitrary"),
        name="diff_attention",
    )(qkv, qkv, qkv, qkv, qkv, lam_params, sub_norm.reshape(1, vd))


def _fox_gate_kernel(fl_ref, b_ref, o_ref, carry_ref, *, tb):
    @pl.when(pl.program_id(0) == 0)
    def _():
        carry_ref[...] = jnp.zeros(carry_ref.shape, F32)

    log_f = jax.nn.log_sigmoid(fl_ref[...] + b_ref[...])
    tri = (lax.broadcasted_iota(I32, (tb, tb), 1) <= lax.broadcasted_iota(I32, (tb, tb), 0)).astype(F32)
    cum = jnp.dot(tri, log_f, preferred_element_type=F32,
                  precision=lax.Precision.HIGHEST) + carry_ref[...]
    carry_ref[...] = cum[tb - 1:tb, :]
    o_ref[...] = cum.T


def _fox_cum_log_forget(f_logit, bias, tb=256):
    s = f_logit.shape[0]
    return pl.pallas_call(
        functools.partial(_fox_gate_kernel, tb=tb),
        grid=(s // tb,),
        in_specs=[pl.BlockSpec((tb, LANES), lambda i: (i, 0)),
                  pl.BlockSpec((1, LANES), lambda i: (0, 0))],
        out_specs=pl.BlockSpec((LANES, tb), lambda i: (0, i)),
        out_shape=jax.ShapeDtypeStruct((LANES, s), F32),
        scratch_shapes=[pltpu.VMEM((1, LANES), F32)],
        compiler_params=_cparams("arbitrary"),
        name="fox_cum_log_forget",
    )(f_logit, bias)


def _fox_attn_kernel(q_ref, k_ref, v_ref, gate_ref, ck_ref, o_ref, m_ref, l_ref, acc_ref, *, tq, tk):
    q0 = pl.multiple_of(pl.program_id(1) * tq, tq)
    m_ref[...] = jnp.full(m_ref.shape, -jnp.inf, F32)
    l_ref[...] = jnp.zeros(l_ref.shape, F32)
    acc_ref[...] = jnp.zeros(acc_ref.shape, F32)
    q = q_ref[...]
    base = ck_ref[:, pl.ds(q0, LANES)][:, 0:1]

    def step(j, masked):
        k0 = pl.multiple_of(j * tk, tk)
        s = _dot_nt(q, k_ref[pl.ds(k0, tk), :]) + (base - ck_ref[:, pl.ds(k0, tk)])
        if masked:
            row = lax.broadcasted_iota(I32, (tq, tk), 0)
            col = lax.broadcasted_iota(I32, (tq, tk), 1)
            s = jnp.where(col <= row, s, -jnp.inf)
        m_prev = m_ref[...]
        m_new = jnp.maximum(m_prev, jnp.max(s, axis=-1, keepdims=True))
        alpha = jnp.exp(m_prev - m_new)
        p = jnp.exp(s - m_new)
        l_ref[...] = alpha * l_ref[...] + jnp.sum(p, axis=-1, keepdims=True)
        acc_ref[...] = alpha * acc_ref[...] + _dot(p.astype(BF16), v_ref[pl.ds(k0, tk), :])
        m_ref[...] = m_new

    n_full = q0 // tk

    def body(j, carry):
        step(j, False)
        return carry

    lax.fori_loop(0, n_full, body, 0)
    step(n_full, True)

    o = acc_ref[...] / l_ref[...] * jax.nn.sigmoid(gate_ref[...].astype(F32))
    o_ref[...] = o.astype(o_ref.dtype)


def _fox_attention(qkvg, cum_t, heads, tq=512):
    s = qkvg.shape[0]
    tk = tq
    return pl.pallas_call(
        functools.partial(_fox_attn_kernel, tq=tq, tk=tk),
        grid=(heads, s // tq),
        in_specs=[
            pl.BlockSpec((tq, HEAD), lambda h, i: (i, h)),
            pl.BlockSpec((s, HEAD), lambda h, i: (0, heads + h)),
            pl.BlockSpec((s, HEAD), lambda h, i: (0, 2 * heads + h)),
            pl.BlockSpec((tq, HEAD), lambda h, i: (i, 3 * heads + h)),
            pl.BlockSpec((None, 1, s), lambda h, i: (h, 0, 0)),
        ],
        out_specs=pl.BlockSpec((tq, HEAD), lambda h, i: (i, h)),
        out_shape=jax.ShapeDtypeStruct((s, heads * HEAD), BF16),
        scratch_shapes=[pltpu.VMEM((tq, 1), F32), pltpu.VMEM((tq, 1), F32),
                        pltpu.VMEM((tq, HEAD), F32)],
        compiler_params=_cparams("parallel", "arbitrary"),
        name="fox_attention",
    )(qkvg, qkvg, qkvg, qkvg, cum_t)


def _gdn_conv_kernel(x_ref, halo_ref, w_ref, o_ref, *, tn, qk_blocks, q_blocks, halo_rows):
    m = pl.program_id(0)
    n = pl.program_id(1)
    x = x_ref[...].astype(F32)
    halo = jnp.where(m > 0, halo_ref[...].astype(F32), 0.0)
    w = w_ref[...]
    taps = w.shape[0]
    x_top = x[0:8, :]
    row8 = lax.broadcasted_iota(I32, (8, tn), 0)
    y = x * w[taps - 1:taps, :]
    y_top = x_top * w[taps - 1:taps, :]
    for sh in range(1, taps):
        wj = w[taps - 1 - sh:taps - sh, :]
        y = y + pltpu.roll(x, sh, 0) * wj
        before = pltpu.roll(halo, sh, 0)[0:8, :]
        y_top = y_top + jnp.where(row8 < sh, before, pltpu.roll(x_top, sh, 0)) * wj

    def emit(transform):
        for rows, a in ((slice(None), y), (slice(0, 8), y_top)):
            o_ref[rows, :] = transform(a * jax.nn.sigmoid(a)).astype(o_ref.dtype)

    def l2_heads(scale):
        def transform(a):
            parts = []
            for c in range(tn // HEAD):
                b = a[:, c * HEAD:(c + 1) * HEAD]
                parts.append(b * (lax.rsqrt(jnp.sum(b * b, axis=-1, keepdims=True) + EPS) * scale))
            return jnp.concatenate(parts, axis=1)
        return transform

    @pl.when(n < q_blocks)
    def _():
        emit(l2_heads(HEAD ** -0.5))

    @pl.when((n >= q_blocks) & (n < qk_blocks))
    def _():
        emit(l2_heads(1.0))

    @pl.when(n >= qk_blocks)
    def _():
        emit(lambda a: a)


def _gdn_conv(proj, conv_w, qk_cols, q_cols, tm=512, tn=512):
    s = proj.shape[0]
    taps, c = conv_w.shape
    halo_rows = 16
    assert tm % halo_rows == 0 and c % tn == 0 and qk_cols % tn == 0 and q_cols % tn == 0
    return pl.pallas_call(
        functools.partial(_gdn_conv_kernel, tn=tn, qk_blocks=qk_cols // tn, q_blocks=q_cols // tn,
                          halo_rows=halo_rows),
        grid=(s // tm, c // tn),
        in_specs=[
            pl.BlockSpec((tm, tn), lambda m, j: (m, j)),
            pl.BlockSpec((halo_rows, tn), lambda m, j: (jnp.maximum(m * (tm // halo_rows) - 1, 0), j)),
            pl.BlockSpec((taps, tn), lambda m, j: (0, j)),
        ],
        out_specs=pl.BlockSpec((tm, tn), lambda m, j: (m, j)),
        out_shape=jax.ShapeDtypeStruct((s, c), BF16),
        compiler_params=_cparams("parallel", "arbitrary"),
        name="gdn_conv",
    )(proj, proj, conv_w)


def _gdn_gate_kernel(x_ref, alog_ref, dtb_ref, o_ref, *, v_heads):
    x = x_ref[...]
    beta = jax.nn.sigmoid(x)
    g = -jnp.exp(alog_ref[...]) * jax.nn.softplus(x + dtb_ref[...])
    lane = lax.broadcasted_iota(I32, x.shape, 1)
    o_ref[...] = jnp.where(lane < v_heads, beta, g).T


def _gdn_gates(ba, a_log, dt_bias, v_heads, tb=512):
    s = ba.shape[0]
    pad = LANES - 2 * v_heads
    alog = jnp.concatenate([jnp.zeros((v_heads,), F32), a_log, jnp.zeros((pad,), F32)]).reshape(1, LANES)
    dtb = jnp.concatenate([jnp.zeros((v_heads,), F32), dt_bias, jnp.zeros((pad,), F32)]).reshape(1, LANES)
    return pl.pallas_call(
        functools.partial(_gdn_gate_kernel, v_heads=v_heads),
        grid=(s // tb,),
        in_specs=[pl.BlockSpec((tb, LANES), lambda i: (i, 0)),
                  pl.BlockSpec((1, LANES), lambda i: (0, 0)),
                  pl.BlockSpec((1, LANES), lambda i: (0, 0))],
        out_specs=pl.BlockSpec((LANES, tb), lambda i: (0, i)),
        out_shape=jax.ShapeDtypeStruct((LANES, s), F32),
        compiler_params=_cparams("parallel"),
        name="gdn_gates",
    )(ba, alog, dtb)


def _unit_lower_inverse(low, ii, jj):
    c = low.shape[0]
    xor = ii ^ jj
    eye = (ii == jj).astype(F32)
    x = eye - jnp.where(xor < 2, low, 0.0)
    b = 2
    while b < c:
        off = jnp.where((xor >= b) & (xor < 2 * b), low, 0.0).astype(BF16)
        xb = x.astype(BF16)
        x = x - _dot(_dot(xb, off).astype(BF16), xb)
        b *= 2
    return x


def _gdn_delta_kernel(q_ref, k_ref, v_ref, z_ref, beta_ref, g_ref, gain_ref, o_ref, state_ref, *,
                      n_chunks, rep):
    c_len = DELTA_CHUNK

    @pl.when(pl.program_id(1) == 0)
    def _():
        state_ref[...] = jnp.zeros(state_ref.shape, F32)

    ii = lax.broadcasted_iota(I32, (c_len, c_len), 0)
    jj = lax.broadcasted_iota(I32, (c_len, c_len), 1)
    causal = jj <= ii
    strict = jj < ii
    diag = jj == ii
    gain = gain_ref[...]

    def chunk(c, carry):
        r0 = pl.multiple_of(c * c_len, c_len)
        q = q_ref[pl.ds(r0, c_len), :]
        k = k_ref[pl.ds(r0, c_len), :]
        kk = _dot_nt(k, k)
        qk = _dot_nt(q, k)
        kf = k.astype(F32)
        qf = q.astype(F32)
        k_t = kf.T
        for e in range(rep):
            g_row = g_ref[e, c]
            b_row = beta_ref[e, c]
            gc_col = jnp.sum(jnp.where(causal, g_row, 0.0), axis=1, keepdims=True)
            gc_row = jnp.sum(jnp.where(diag, gc_col, 0.0), axis=0, keepdims=True)
            b_col = jnp.sum(jnp.where(diag, b_row, 0.0), axis=1, keepdims=True)
            g_last = jnp.sum(g_row, axis=1, keepdims=True)
            decay = jnp.where(causal, jnp.exp(gc_col - gc_row), 0.0)
            low = jnp.where(strict, b_col * kk * decay, 0.0)
            t_inv = _unit_lower_inverse(low, ii, jj)
            e_gc = jnp.exp(gc_col)
            v = v_ref[pl.ds(r0, c_len), e * HEAD:(e + 1) * HEAD].astype(F32)
            kb = kf * b_col
            rhs = jnp.concatenate([v * b_col, kb * e_gc], axis=1).astype(BF16)
            sol = _dot(t_inv.astype(BF16), rhs)
            u = sol[:, :HEAD]
            w = sol[:, HEAD:]
            intra = (qk * decay).astype(BF16)
            q_dec = qf * e_gc
            k_dec_t = (k_t * jnp.exp(g_last - gc_row)).astype(BF16)
            state = state_ref[e]
            ws_qs = _dot(jnp.concatenate([w, q_dec], axis=0).astype(BF16), state.astype(BF16))
            v_new = u - ws_qs[:c_len]
            v_new_b = v_new.astype(BF16)
            o = ws_qs[c_len:] + _dot(intra, v_new_b)
            state_ref[e] = state * jnp.exp(g_last) + _dot(k_dec_t, v_new_b)
            z = z_ref[pl.ds(r0, c_len), e * HEAD:(e + 1) * HEAD].astype(F32)
            o = _rms_rows(o) * gain * (z * jax.nn.sigmoid(z))
            o_ref[pl.ds(r0, c_len), e * HEAD:(e + 1) * HEAD] = o.astype(o_ref.dtype)
        return carry

    lax.fori_loop(0, n_chunks, chunk, 0)


def _gdn_delta(qkv, proj, gates, out_gain, qk_heads, v_heads, z_col0, tt=1024):
    s = qkv.shape[0]
    rep = v_heads // qk_heads
    c_len = DELTA_CHUNK
    n_chunks = tt // c_len
    g4 = gates.reshape(LANES, s // c_len, 1, c_len)
    vw = rep * HEAD
    return pl.pallas_call(
        functools.partial(_gdn_delta_kernel, n_chunks=n_chunks, rep=rep),
        grid=(qk_heads, s // tt),
        in_specs=[
            pl.BlockSpec((tt, HEAD), lambda h, t: (t, h)),
            pl.BlockSpec((tt, HEAD), lambda h, t: (t, qk_heads + h)),
            pl.BlockSpec((tt, vw), lambda h, t: (t, (2 * qk_heads * HEAD) // vw + h)),
            pl.BlockSpec((tt, vw), lambda h, t: (t, z_col0 // vw + h)),
            pl.BlockSpec((rep, n_chunks, 1, c_len), lambda h, t: (h, t, 0, 0)),
            pl.BlockSpec((rep, n_chunks, 1, c_len), lambda h, t: (v_heads // rep + h, t, 0, 0)),
            pl.BlockSpec((1, HEAD), lambda h, t: (0, 0)),
        ],
        out_specs=pl.BlockSpec((tt, vw), lambda h, t: (t, h)),
        out_shape=jax.ShapeDtypeStruct((s, v_heads * HEAD), BF16),
        scratch_shapes=[pltpu.VMEM((rep, HEAD, HEAD), F32)],
        compiler_params=_cparams("parallel", "arbitrary"),
        name="gdn_delta",
    )(qkv, qkv, qkv, proj, g4, g4, out_gain.reshape(1, HEAD))


def _pad_cols(w, width):
    return jnp.pad(w, ((0, 0), (0, width - w.shape[1])))


def _mixer_a(h, gain, w_in, w_out, lam_q1, lam_k1, lam_q2, lam_k2, sub_norm, lambda_init):
    d = h.shape[1]
    heads = d // (2 * HEAD)
    qk_cols = 2 * heads * HEAD
    colscale = jnp.concatenate([jnp.full((qk_cols,), HEAD ** -0.5, F32),
                                jnp.ones((w_in.shape[1] - qk_cols,), F32)])
    qkv = _norm_proj(h, gain, w_in.astype(BF16), colscale=colscale)
    lam_params = jnp.stack([lam_q1, lam_k1, lam_q2, lam_k2]).astype(F32)
    o = _diff_attention(qkv, lam_params, sub_norm, lambda_init, heads)
    return _out_proj(o, w_out.astype(BF16), h)


def _mixer_b(h, gain, w_in, w_out, forget_bias, q_norm, k_norm):
    d = h.shape[1]
    heads = d // HEAD
    width = heads * HEAD
    colscale = jnp.concatenate([jnp.tile(q_norm, heads) * HEAD ** -0.5, jnp.tile(k_norm, heads),
                                jnp.ones((2 * width,), F32)])
    qkvg = _norm_proj(h, gain, w_in[:, :4 * width].astype(BF16), colscale=colscale, norm_cols=2 * width)
    f_logit = _norm_proj(h, gain, _pad_cols(w_in[:, 4 * width:], LANES).astype(BF16), out_dtype=F32)
    bias = jnp.pad(forget_bias, (0, LANES - heads)).reshape(1, LANES)
    cum_t = _fox_cum_log_forget(f_logit, bias)[:heads].reshape(heads, 1, -1)
    o = _fox_attention(qkvg, cum_t, heads)
    return _out_proj(o, w_out.astype(BF16), h)


def _mixer_c(h, gain, w_in, w_out, conv_w, a_log, dt_bias, out_norm):
    d = h.shape[1]
    qk_heads = d // HEAD
    v_heads = 2 * qk_heads
    qk_w = qk_heads * HEAD
    v_w = v_heads * HEAD
    main = 2 * qk_w + 2 * v_w
    proj = _norm_proj(h, gain, w_in[:, :main].astype(BF16))
    ba = _norm_proj(h, gain, _pad_cols(w_in[:, main:], LANES).astype(BF16), out_dtype=F32)
    gates = _gdn_gates(ba, a_log, dt_bias, v_heads)
    qkv = _gdn_conv(proj, conv_w, qk_cols=2 * qk_w, q_cols=qk_w)
    o = _gdn_delta(qkv, proj, gates, out_norm, qk_heads, v_heads, z_col0=2 * qk_w + v_w)
    return _out_proj(o, w_out.astype(BF16), h)


def kernel(x, mix_norm, ffn_norm, final_norm, a_w_in, a_w_out, a_lam_q1, a_lam_k1, a_lam_q2, a_lam_k2,
           a_sub_norm, b_w_in, b_w_out, b_forget_bias, b_q_norm, b_k_norm, c_w_in, c_w_out, c_conv_w,
           c_a_log, c_dt_bias, c_out_norm, ffn_w_gate, ffn_w_up, ffn_w_down):
    batch, seq, d = x.shape
    depth = mix_norm.shape[0]
    outs = []
    for b in range(batch):
        h = x[b]
        for i in range(depth):
            slot = i // N_MIXERS
            if i % N_MIXERS == 0:
                lambda_init = 0.8 - 0.6 * math.exp(-0.3 * i)
                h = _mixer_a(h, mix_norm[i], a_w_in[slot], a_w_out[slot], a_lam_q1[slot], a_lam_k1[slot],
                             a_lam_q2[slot], a_lam_k2[slot], a_sub_norm[slot], lambda_init)
            elif i % N_MIXERS == 1:
                h = _mixer_b(h, mix_norm[i], b_w_in[slot], b_w_out[slot], b_forget_bias[slot],
                             b_q_norm[slot], b_k_norm[slot])
            else:
                h = _mixer_c(h, mix_norm[i], c_w_in[slot], c_w_out[slot], c_conv_w[slot], c_a_log[slot],
                             c_dt_bias[slot], c_out_norm[slot])
            h = _ffn(h, ffn_norm[i], ffn_w_gate[i].astype(BF16), ffn_w_up[i].astype(BF16),
                     ffn_w_down[i].astype(BF16), final_gain=final_norm if i == depth - 1 else None)
        outs.append(h)
    return jnp.stack(outs)
```

```python
import functools
import math

import jax
import jax.numpy as jnp
from jax import lax
from jax.experimental import pallas as pl
from jax.experimental.pallas import tpu as pltpu

F32 = jnp.float32
BF16 = jnp.bfloat16
I32 = jnp.int32

EPS = 1e-6
MASK_CHUNK = 64
HEAD = 128
LANES = 128
N_MIXERS = 3
LOG2_E = math.log2(math.e)
DELTA_CHUNK = 128
VMEM_LIMIT_BYTES = 52 * 1024 * 1024


def _cparams(*sem):
    return pltpu.CompilerParams(dimension_semantics=sem, vmem_limit_bytes=VMEM_LIMIT_BYTES)


def _dot(a, b):
    return jnp.dot(a, b, preferred_element_type=F32)


def _dot_nt(a, b):
    return lax.dot_general(a, b, (((1,), (1,)), ((), ())), preferred_element_type=F32)


def _rms_rows(x):
    return x * lax.rsqrt(jnp.mean(x * x, axis=-1, keepdims=True) + EPS)


def _proj_kernel(*refs, tn, norm_blocks, has_colscale):
    if has_colscale:
        x_ref, g_ref, w_ref, cs_ref, o_ref, xn_ref = refs
    else:
        x_ref, g_ref, w_ref, o_ref, xn_ref = refs
        cs_ref = None
    n = pl.program_id(1)

    @pl.when(n == 0)
    def _():
        xn_ref[...] = (_rms_rows(x_ref[...]) * g_ref[...]).astype(BF16)

    acc = _dot(xn_ref[...], w_ref[...])

    def finish(a):
        if cs_ref is not None:
            a = a * cs_ref[...]
        o_ref[...] = a.astype(o_ref.dtype)

    if norm_blocks == 0:
        finish(acc)
    else:
        @pl.when(n < norm_blocks)
        def _():
            parts = [_rms_rows(acc[:, c * HEAD:(c + 1) * HEAD]) for c in range(tn // HEAD)]
            finish(jnp.concatenate(parts, axis=1))

        @pl.when(n >= norm_blocks)
        def _():
            finish(acc)


def _norm_proj(h, gain, w, colscale=None, norm_cols=0, out_dtype=BF16, tm=512, tn=512):
    s, d = h.shape
    n = w.shape[1]
    tn = min(tn, n)
    assert s % tm == 0 and n % tn == 0 and norm_cols % tn == 0 and tn % HEAD == 0
    in_specs = [
        pl.BlockSpec((tm, d), lambda m, j: (m, 0)),
        pl.BlockSpec((1, d), lambda m, j: (0, 0)),
        pl.BlockSpec((d, tn), lambda m, j: (0, j)),
    ]
    args = [h, gain.reshape(1, d), w]
    if colscale is not None:
        in_specs.append(pl.BlockSpec((1, tn), lambda m, j: (0, j)))
        args.append(colscale.reshape(1, n))
    return pl.pallas_call(
        functools.partial(_proj_kernel, tn=tn, norm_blocks=norm_cols // tn,
                          has_colscale=colscale is not None),
        grid=(s // tm, n // tn),
        in_specs=in_specs,
        out_specs=pl.BlockSpec((tm, tn), lambda m, j: (m, j)),
        out_shape=jax.ShapeDtypeStruct((s, n), out_dtype),
        scratch_shapes=[pltpu.VMEM((tm, d), BF16)],
        compiler_params=_cparams("parallel", "arbitrary"),
        name="norm_proj",
    )(*args)


def _out_proj_kernel(o_ref, w_ref, h_ref, y_ref):
    y_ref[...] = h_ref[...] + _dot(o_ref[...], w_ref[...])


def _out_proj(o, w, h, tm=512, tn=512):
    s, k = o.shape
    d = w.shape[1]
    return pl.pallas_call(
        _out_proj_kernel,
        grid=(s // tm, d // tn),
        in_specs=[
            pl.BlockSpec((tm, k), lambda m, j: (m, 0)),
            pl.BlockSpec((k, tn), lambda m, j: (0, j)),
            pl.BlockSpec((tm, tn), lambda m, j: (m, j)),
        ],
        out_specs=pl.BlockSpec((tm, tn), lambda m, j: (m, j)),
        out_shape=jax.ShapeDtypeStruct((s, d), F32),
        compiler_params=_cparams("parallel", "arbitrary"),
        name="out_proj",
    )(o, w, h)


def _ffn_kernel(*refs, has_final):
    if has_final:
        h_ref, g_ref, wg_ref, wu_ref, wd_ref, fg_ref, y_ref, xn_ref = refs
    else:
        h_ref, g_ref, wg_ref, wu_ref, wd_ref, y_ref, xn_ref = refs
        fg_ref = None
    f = pl.program_id(1)

    @pl.when(f == 0)
    def _():
        h = h_ref[...]
        xn_ref[...] = (_rms_rows(h) * g_ref[...]).astype(BF16)
        y_ref[...] = h

    xn = xn_ref[...]
    gate = _dot(xn, wg_ref[...])
    up = _dot(xn, wu_ref[...])
    act = (gate * jax.nn.sigmoid(gate) * up).astype(BF16)
    y_ref[...] += _dot(act, wd_ref[...])

    if fg_ref is not None:
        @pl.when(f == pl.num_programs(1) - 1)
        def _():
            y_ref[...] = _rms_rows(y_ref[...]) * fg_ref[...]


def _ffn(h, gain, w_gate, w_up, w_down, final_gain=None, tm=512, tf=512):
    s, d = h.shape
    dff = w_gate.shape[1]
    in_specs = [
        pl.BlockSpec((tm, d), lambda m, f: (m, 0)),
        pl.BlockSpec((1, d), lambda m, f: (0, 0)),
        pl.BlockSpec((d, tf), lambda m, f: (0, f)),
        pl.BlockSpec((d, tf), lambda m, f: (0, f)),
        pl.BlockSpec((tf, d), lambda m, f: (f, 0)),
    ]
    args = [h, gain.reshape(1, d), w_gate, w_up, w_down]
    if final_gain is not None:
        in_specs.append(pl.BlockSpec((1, d), lambda m, f: (0, 0)))
        args.append(final_gain.reshape(1, d))
    return pl.pallas_call(
        functools.partial(_ffn_kernel, has_final=final_gain is not None),
        grid=(s // tm, dff // tf),
        in_specs=in_specs,
        out_specs=pl.BlockSpec((tm, d), lambda m, f: (m, 0)),
        out_shape=jax.ShapeDtypeStruct((s, d), F32),
        scratch_shapes=[pltpu.VMEM((tm, d), BF16)],
        compiler_params=_cparams("parallel", "arbitrary"),
        name="swiglu",
    )(*args)


def _kv_block_start(t, q0, tk):
    return pl.multiple_of(jnp.where(t < 2, q0 + t * tk, (t - 2) * tk), tk)


def _run_kv_pipeline(n_full_pairs, scores, softmax, weighted_values):
    scores(0, 0, True)
    scores(1, 1, True)
    softmax(0)

    def body(u, carry):
        t = 2 * u
        scores(t + 2, 0, False)
        softmax(1)
        weighted_values(t, 0)
        scores(t + 3, 1, False)
        softmax(0)
        weighted_values(t + 1, 1)
        return carry

    lax.fori_loop(0, n_full_pairs, body, 0)
    n = 2 + 2 * n_full_pairs
    softmax(1)
    weighted_values(n - 2, 0)
    weighted_values(n - 1, 1)


def _diff_attn_kernel(q1_ref, q2_ref, k1_ref, k2_ref, v_ref, lam_ref, sn_ref, o_ref,
                      s_ref, p_ref, a_ref, m_ref, l_ref, acc_ref, *, tq, tk, lambda_init):
    i = pl.program_id(1)
    q0 = pl.multiple_of(i * tq, tq)
    lane_tiles = tk // LANES
    v_tiles = v_ref.shape[1] // LANES
    m_ref[...] = jnp.full(m_ref.shape, -jnp.inf, F32)
    l_ref[...] = jnp.zeros(l_ref.shape, F32)
    acc_ref[...] = jnp.zeros(acc_ref.shape, F32)
    qs = (q1_ref[...], q2_ref[...])
    ks = (k1_ref, k2_ref)

    def scores(t, slot, masked):
        k0 = _kv_block_start(t, q0, tk)
        if masked:
            row = lax.broadcasted_iota(I32, (tq, tk), 0)
            col = lax.broadcasted_iota(I32, (tq, tk), 1) + (k0 - q0)
            visible = (col // MASK_CHUNK) <= (row // MASK_CHUNK)
        for mp in range(2):
            s = _dot_nt(qs[mp], ks[mp][pl.ds(k0, tk), :])
            if masked:
                s = jnp.where(visible, s, -jnp.inf)
            s_ref[slot, mp] = s

    def softmax(slot):
        for mp in range(2):
            s = s_ref[slot, mp]
            m_prev = m_ref[mp]
            m_new = jnp.maximum(m_prev, jnp.max(s, axis=-1, keepdims=True))
            alpha = jnp.exp2(m_prev - m_new)
            l_new = alpha * l_ref[mp]
            for c in range(lane_tiles):
                p = jnp.exp2(s[:, c * LANES:(c + 1) * LANES] - m_new)
                l_new = l_new + p
                p_ref[slot, mp, :, c * LANES:(c + 1) * LANES] = p.astype(BF16)
            l_ref[mp] = l_new
            m_ref[mp] = m_new
            a_ref[slot, mp] = alpha

    def weighted_values(t, slot):
        k0 = _kv_block_start(t, q0, tk)
        v = v_ref[pl.ds(k0, tk), :]
        for mp in range(2):
            alpha = jnp.concatenate([a_ref[slot, mp]] * v_tiles, axis=1)
            acc_ref[mp] = alpha * acc_ref[mp] + _dot(p_ref[slot, mp], v)

    _run_kv_pipeline(i * (tq // (2 * tk)), scores, softmax, weighted_values)

    lam = (jnp.exp(jnp.sum(lam_ref[0:1, :] * lam_ref[1:2, :], keepdims=True))
           - jnp.exp(jnp.sum(lam_ref[2:3, :] * lam_ref[3:4, :], keepdims=True)) + lambda_init)
    l1 = jnp.sum(l_ref[0], axis=-1, keepdims=True)
    l2 = jnp.sum(l_ref[1], axis=-1, keepdims=True)
    o = acc_ref[0] / l1 - lam * (acc_ref[1] / l2)
    o = _rms_rows(o) * sn_ref[...] * (1.0 - lambda_init)
    o_ref[...] = o.astype(o_ref.dtype)


def _diff_attention(qkv, lam_params, sub_norm, lambda_init, heads, tq=512):
    s = qkv.shape[0]
    vd = 2 * HEAD
    tk = tq // 2
    hb = heads
    return pl.pallas_call(
        functools.partial(_diff_attn_kernel, tq=tq, tk=tk, lambda_init=lambda_init),
        grid=(heads, s // tq),
        in_specs=[
            pl.BlockSpec((tq, HEAD), lambda h, i: (i, h)),
            pl.BlockSpec((tq, HEAD), lambda h, i: (i, hb + h)),
            pl.BlockSpec((s, HEAD), lambda h, i: (0, 2 * hb + h)),
            pl.BlockSpec((s, HEAD), lambda h, i: (0, 3 * hb + h)),
            pl.BlockSpec((s, vd), lambda h, i: (0, 2 * hb + h)),
            pl.BlockSpec((4, HEAD), lambda h, i: (0, 0)),
            pl.BlockSpec((1, vd), lambda h, i: (0, 0)),
        ],
        out_specs=pl.BlockSpec((tq, vd), lambda h, i: (i, h)),
        out_shape=jax.ShapeDtypeStruct((s, heads * vd), BF16),
        scratch_shapes=[pltpu.VMEM((2, 2, tq, tk), F32), pltpu.VMEM((2, 2, tq, tk), BF16),
                        pltpu.VMEM((2, 2, tq, LANES), F32), pltpu.VMEM((2, tq, LANES), F32),
                        pltpu.VMEM((2, tq, LANES), F32), pltpu.VMEM((2, tq, vd), F32)],
        compiler_params=_cparams("parallel", "arbitrary"),
        name="diff_attention",
    )(qkv, qkv, qkv, qkv, qkv, lam_params, sub_norm.reshape(1, vd))


def _fox_gate_kernel(fl_ref, b_ref, o_ref, carry_ref, *, tb):
    @pl.when(pl.program_id(0) == 0)
    def _():
        carry_ref[...] = jnp.zeros(carry_ref.shape, F32)

    log_f = jax.nn.log_sigmoid(fl_ref[...] + b_ref[...])
    tri = (lax.broadcasted_iota(I32, (tb, tb), 1) <= lax.broadcasted_iota(I32, (tb, tb), 0)).astype(F32)
    cum = jnp.dot(tri, log_f, preferred_element_type=F32,
                  precision=lax.Precision.HIGHEST) + carry_ref[...]
    carry_ref[...] = cum[tb - 1:tb, :]
    o_ref[...] = cum.T


def _fox_cum_log_forget(f_logit, bias, tb=256):
    s = f_logit.shape[0]
    return pl.pallas_call(
        functools.partial(_fox_gate_kernel, tb=tb),
        grid=(s // tb,),
        in_specs=[pl.BlockSpec((tb, LANES), lambda i: (i, 0)),
                  pl.BlockSpec((1, LANES), lambda i: (0, 0))],
        out_specs=pl.BlockSpec((LANES, tb), lambda i: (0, i)),
        out_shape=jax.ShapeDtypeStruct((LANES, s), F32),
        scratch_shapes=[pltpu.VMEM((1, LANES), F32)],
        compiler_params=_cparams("arbitrary"),
        name="fox_cum_log_forget",
    )(f_logit, bias)


def _fox_attn_kernel(q_ref, k_ref, v_ref, gate_ref, ck_ref, o_ref,
                     s_ref, p_ref, a_ref, m_ref, l_ref, acc_ref, *, tq, tk):
    i = pl.program_id(1)
    q0 = pl.multiple_of(i * tq, tq)
    lane_tiles = tk // LANES
    m_ref[...] = jnp.full(m_ref.shape, -jnp.inf, F32)
    l_ref[...] = jnp.zeros(l_ref.shape, F32)
    acc_ref[...] = jnp.zeros(acc_ref.shape, F32)
    q = q_ref[...]
    base = ck_ref[:, pl.ds(q0, LANES)][:, 0:1]

    def scores(t, slot, masked):
        k0 = _kv_block_start(t, q0, tk)
        s = _dot_nt(q, k_ref[pl.ds(k0, tk), :]) + (base - ck_ref[:, pl.ds(k0, tk)]) * LOG2_E
        if masked:
            row = lax.broadcasted_iota(I32, (tq, tk), 0)
            col = lax.broadcasted_iota(I32, (tq, tk), 1)
            s = jnp.where(col + (k0 - q0) <= row, s, -jnp.inf)
        s_ref[slot] = s

    def softmax(slot):
        s = s_ref[slot]
        m_prev = m_ref[...]
        m_new = jnp.maximum(m_prev, jnp.max(s, axis=-1, keepdims=True))
        alpha = jnp.exp2(m_prev - m_new)
        l_new = alpha * l_ref[...]
        for c in range(lane_tiles):
            p = jnp.exp2(s[:, c * LANES:(c + 1) * LANES] - m_new)
            l_new = l_new + p
            p_ref[slot, :, c * LANES:(c + 1) * LANES] = p.astype(BF16)
        l_ref[...] = l_new
        m_ref[...] = m_new
        a_ref[slot] = alpha

    def weighted_values(t, slot):
        k0 = _kv_block_start(t, q0, tk)
        acc_ref[...] = a_ref[slot] * acc_ref[...] + _dot(p_ref[slot], v_ref[pl.ds(k0, tk), :])

    _run_kv_pipeline(i * (tq // (2 * tk)), scores, softmax, weighted_values)

    l = jnp.sum(l_ref[...], axis=-1, keepdims=True)
    o = acc_ref[...] / l * jax.nn.sigmoid(gate_ref[...].astype(F32))
    o_ref[...] = o.astype(o_ref.dtype)


def _fox_attention(qkvg, cum_t, heads, tq=512):
    s = qkvg.shape[0]
    tk = tq // 2
    return pl.pallas_call(
        functools.partial(_fox_attn_kernel, tq=tq, tk=tk),
        grid=(heads, s // tq),
        in_specs=[
            pl.BlockSpec((tq, HEAD), lambda h, i: (i, h)),
            pl.BlockSpec((s, HEAD), lambda h, i: (0, heads + h)),
            pl.BlockSpec((s, HEAD), lambda h, i: (0, 2 * heads + h)),
            pl.BlockSpec((tq, HEAD), lambda h, i: (i, 3 * heads + h)),
            pl.BlockSpec((None, 1, s), lambda h, i: (h, 0, 0)),
        ],
        out_specs=pl.BlockSpec((tq, HEAD), lambda h, i: (i, h)),
        out_shape=jax.ShapeDtypeStruct((s, heads * HEAD), BF16),
        scratch_shapes=[pltpu.VMEM((2, tq, tk), F32), pltpu.VMEM((2, tq, tk), BF16),
                        pltpu.VMEM((2, tq, LANES), F32), pltpu.VMEM((tq, LANES), F32),
                        pltpu.VMEM((tq, LANES), F32), pltpu.VMEM((tq, HEAD), F32)],
        compiler_params=_cparams("parallel", "arbitrary"),
        name="fox_attention",
    )(qkvg, qkvg, qkvg, qkvg, cum_t)


def _gdn_conv_kernel(x_ref, halo_ref, w_ref, o_ref, *, tn, qk_blocks, q_blocks, halo_rows):
    m = pl.program_id(0)
    n = pl.program_id(1)
    x = x_ref[...].astype(F32)
    halo = jnp.where(m > 0, halo_ref[...].astype(F32), 0.0)
    w = w_ref[...]
    taps = w.shape[0]
    x_top = x[0:8, :]
    row8 = lax.broadcasted_iota(I32, (8, tn), 0)
    y = x * w[taps - 1:taps, :]
    y_top = x_top * w[taps - 1:taps, :]
    for sh in range(1, taps):
        wj = w[taps - 1 - sh:taps - sh, :]
        y = y + pltpu.roll(x, sh, 0) * wj
        before = pltpu.roll(halo, sh, 0)[0:8, :]
        y_top = y_top + jnp.where(row8 < sh, before, pltpu.roll(x_top, sh, 0)) * wj

    def emit(transform):
        for rows, a in ((slice(None), y), (slice(0, 8), y_top)):
            o_ref[rows, :] = transform(a * jax.nn.sigmoid(a)).astype(o_ref.dtype)

    def l2_heads(scale):
        def transform(a):
            parts = []
            for c in range(tn // HEAD):
                b = a[:, c * HEAD:(c + 1) * HEAD]
                parts.append(b * (lax.rsqrt(jnp.sum(b * b, axis=-1, keepdims=True) + EPS) * scale))
            return jnp.concatenate(parts, axis=1)
        return transform

    @pl.when(n < q_blocks)
    def _():
        emit(l2_heads(HEAD ** -0.5))

    @pl.when((n >= q_blocks) & (n < qk_blocks))
    def _():
        emit(l2_heads(1.0))

    @pl.when(n >= qk_blocks)
    def _():
        emit(lambda a: a)


def _gdn_conv(proj, conv_w, qk_cols, q_cols, tm=512, tn=512):
    s = proj.shape[0]
    taps, c = conv_w.shape
    halo_rows = 16
    assert tm % halo_rows == 0 and c % tn == 0 and qk_cols % tn == 0 and q_cols % tn == 0
    return pl.pallas_call(
        functools.partial(_gdn_conv_kernel, tn=tn, qk_blocks=qk_cols // tn, q_blocks=q_cols // tn,
                          halo_rows=halo_rows),
        grid=(s // tm, c // tn),
        in_specs=[
            pl.BlockSpec((tm, tn), lambda m, j: (m, j)),
            pl.BlockSpec((halo_rows, tn), lambda m, j: (jnp.maximum(m * (tm // halo_rows) - 1, 0), j)),
            pl.BlockSpec((taps, tn), lambda m, j: (0, j)),
        ],
        out_specs=pl.BlockSpec((tm, tn), lambda m, j: (m, j)),
        out_shape=jax.ShapeDtypeStruct((s, c), BF16),
        compiler_params=_cparams("parallel", "arbitrary"),
        name="gdn_conv",
    )(proj, proj, conv_w)


def _gdn_gate_kernel(x_ref, alog_ref, dtb_ref, o_ref, *, v_heads):
    x = x_ref[...]
    beta = jax.nn.sigmoid(x)
    g = -jnp.exp(alog_ref[...]) * jax.nn.softplus(x + dtb_ref[...])
    lane = lax.broadcasted_iota(I32, x.shape, 1)
    o_ref[...] = jnp.where(lane < v_heads, beta, g).T


def _gdn_gates(ba, a_log, dt_bias, v_heads, tb=512):
    s = ba.shape[0]
    pad = LANES - 2 * v_heads
    alog = jnp.concatenate([jnp.zeros((v_heads,), F32), a_log, jnp.zeros((pad,), F32)]).reshape(1, LANES)
    dtb = jnp.concatenate([jnp.zeros((v_heads,), F32), dt_bias, jnp.zeros((pad,), F32)]).reshape(1, LANES)
    return pl.pallas_call(
        functools.partial(_gdn_gate_kernel, v_heads=v_heads),
        grid=(s // tb,),
        in_specs=[pl.BlockSpec((tb, LANES), lambda i: (i, 0)),
                  pl.BlockSpec((1, LANES), lambda i: (0, 0)),
                  pl.BlockSpec((1, LANES), lambda i: (0, 0))],
        out_specs=pl.BlockSpec((LANES, tb), lambda i: (0, i)),
        out_shape=jax.ShapeDtypeStruct((LANES, s), F32),
        compiler_params=_cparams("parallel"),
        name="gdn_gates",
    )(ba, alog, dtb)


def _unit_lower_inverses(lows, ii, jj):
    c = lows[0].shape[0]
    xor = ii ^ jj
    eye = (ii == jj).astype(F32)
    xs = [eye - jnp.where(xor < 2, low, 0.0) for low in lows]
    b = 2
    while b < c:
        lower_left = (xor >= b) & (xor < 2 * b)
        offs = [jnp.where(lower_left, low, 0.0).astype(BF16) for low in lows]
        xbs = [x.astype(BF16) for x in xs]
        ys = [_dot(xb, off).astype(BF16) for xb, off in zip(xbs, offs)]
        xs = [x - _dot(y, xb) for x, y, xb in zip(xs, ys, xbs)]
        b *= 2
    return xs


def _gdn_delta_kernel(q_ref, k_ref, v_ref, z_ref, beta_ref, g_ref, gain_ref, o_ref,
                      state_ref, u_ref, wq_ref, intra_ref, kdt_ref, *, n_chunks, rep, par):
    c_len = DELTA_CHUNK

    @pl.when(pl.program_id(1) == 0)
    def _():
        state_ref[...] = jnp.zeros(state_ref.shape, F32)

    def chunk_local(cg, carry):
        ii = lax.broadcasted_iota(I32, (c_len, c_len), 0)
        jj = lax.broadcasted_iota(I32, (c_len, c_len), 1)
        causal = jj <= ii
        strict = jj < ii
        diag = jj == ii
        cs = [cg * par + cc for cc in range(par)]
        rows = [pl.ds(pl.multiple_of(c * c_len, c_len), c_len) for c in cs]
        ks = [k_ref[r, :] for r in rows]
        qs = [q_ref[r, :] for r in rows]
        kks = [_dot_nt(k, k) for k in ks]
        qks = [_dot_nt(q, k) for q, k in zip(qs, ks)]
        kfs = [k.astype(F32) for k in ks]
        kts = [kf.T for kf in kfs]
        chains = [(cc, e) for cc in range(par) for e in range(rep)]
        g_rows = [g_ref[e, cs[cc]] for cc, e in chains]
        b_rows = [beta_ref[e, cs[cc]] for cc, e in chains]
        gc_cols = [jnp.sum(jnp.where(causal, g, 0.0), axis=1, keepdims=True) for g in g_rows]
        gc_rows = [jnp.sum(jnp.where(diag, gc, 0.0), axis=0, keepdims=True) for gc in gc_cols]
        b_cols = [jnp.sum(jnp.where(diag, b, 0.0), axis=1, keepdims=True) for b in b_rows]
        g_lasts = [jnp.sum(g, axis=1, keepdims=True) for g in g_rows]
        decays = [jnp.where(causal, jnp.exp(gc - gr), 0.0) for gc, gr in zip(gc_cols, gc_rows)]
        lows = [jnp.where(strict, bc * kks[cc] * dec, 0.0)
                for (cc, e), bc, dec in zip(chains, b_cols, decays)]
        t_invs = _unit_lower_inverses(lows, ii, jj)
        for n, (cc, e) in enumerate(chains):
            c = cs[cc]
            e_gc = jnp.exp(gc_cols[n])
            v = v_ref[rows[cc], e * HEAD:(e + 1) * HEAD].astype(F32)
            kb = kfs[cc] * b_cols[n]
            rhs = jnp.concatenate([v * b_cols[n], kb * e_gc], axis=1).astype(BF16)
            sol = _dot(t_invs[n].astype(BF16), rhs)
            u_ref[c, e] = sol[:, :HEAD]
            wq_ref[c, e, 0:c_len, :] = sol[:, HEAD:].astype(BF16)
            wq_ref[c, e, c_len:2 * c_len, :] = (qs[cc].astype(F32) * e_gc).astype(BF16)
            intra_ref[c, e] = (qks[cc] * decays[n]).astype(BF16)
            kdt_ref[c, e] = (kts[cc] * jnp.exp(g_lasts[n] - gc_rows[n])).astype(BF16)
        return carry

    lax.fori_loop(0, n_chunks // par, chunk_local, 0)

    gain = gain_ref[...]

    def chunk_recurrent(c, carry):
        rows = pl.ds(pl.multiple_of(c * c_len, c_len), c_len)
        heads = range(rep)
        states = [state_ref[e] for e in heads]
        ws_qs = [_dot(wq_ref[c, e], states[e].astype(BF16)) for e in heads]
        v_news = [(u_ref[c, e] - ws_qs[e][:c_len]).astype(BF16) for e in heads]
        upds = [_dot(kdt_ref[c, e], v_news[e]) for e in heads]
        for e in heads:
            decay_all = jnp.exp(jnp.sum(g_ref[e, c], axis=1, keepdims=True))
            state_ref[e] = states[e] * decay_all + upds[e]
        for e in heads:
            o = ws_qs[e][c_len:] + _dot(intra_ref[c, e], v_news[e])
            z = z_ref[rows, e * HEAD:(e + 1) * HEAD].astype(F32)
            o = _rms_rows(o) * gain * (z * jax.nn.sigmoid(z))
            o_ref[rows, e * HEAD:(e + 1) * HEAD] = o.astype(o_ref.dtype)
        return carry

    lax.fori_loop(0, n_chunks, chunk_recurrent, 0)


def _gdn_delta(qkv, proj, gates, out_gain, qk_heads, v_heads, z_col0, tt=1024, par=8):
    s = qkv.shape[0]
    rep = v_heads // qk_heads
    c_len = DELTA_CHUNK
    n_chunks = tt // c_len
    g4 = gates.reshape(LANES, s // c_len, 1, c_len)
    vw = rep * HEAD
    return pl.pallas_call(
        functools.partial(_gdn_delta_kernel, n_chunks=n_chunks, rep=rep, par=par),
        grid=(qk_heads, s // tt),
        in_specs=[
            pl.BlockSpec((tt, HEAD), lambda h, t: (t, h)),
            pl.BlockSpec((tt, HEAD), lambda h, t: (t, qk_heads + h)),
            pl.BlockSpec((tt, vw), lambda h, t: (t, (2 * qk_heads * HEAD) // vw + h)),
            pl.BlockSpec((tt, vw), lambda h, t: (t, z_col0 // vw + h)),
            pl.BlockSpec((rep, n_chunks, 1, c_len), lambda h, t: (h, t, 0, 0)),
            pl.BlockSpec((rep, n_chunks, 1, c_len), lambda h, t: (v_heads // rep + h, t, 0, 0)),
            pl.BlockSpec((1, HEAD), lambda h, t: (0, 0)),
        ],
        out_specs=pl.BlockSpec((tt, vw), lambda h, t: (t, h)),
        out_shape=jax.ShapeDtypeStruct((s, v_heads * HEAD), BF16),
        scratch_shapes=[pltpu.VMEM((rep, HEAD, HEAD), F32),
                        pltpu.VMEM((n_chunks, rep, c_len, HEAD), F32),
                        pltpu.VMEM((n_chunks, rep, 2 * c_len, HEAD), BF16),
                        pltpu.VMEM((n_chunks, rep, c_len, c_len), BF16),
                        pltpu.VMEM((n_chunks, rep, HEAD, c_len), BF16)],
        compiler_params=_cparams("parallel", "arbitrary"),
        name="gdn_delta",
    )(qkv, qkv, qkv, proj, g4, g4, out_gain.reshape(1, HEAD))


def _pad_cols(w, width):
    return jnp.pad(w, ((0, 0), (0, width - w.shape[1])))


def _mixer_a(h, gain, w_in, w_out, lam_q1, lam_k1, lam_q2, lam_k2, sub_norm, lambda_init):
    d = h.shape[1]
    heads = d // (2 * HEAD)
    qk_cols = 2 * heads * HEAD
    colscale = jnp.concatenate([jnp.full((qk_cols,), LOG2_E * HEAD ** -0.5, F32),
                                jnp.ones((w_in.shape[1] - qk_cols,), F32)])
    qkv = _norm_proj(h, gain, w_in.astype(BF16), colscale=colscale)
    lam_params = jnp.stack([lam_q1, lam_k1, lam_q2, lam_k2]).astype(F32)
    o = _diff_attention(qkv, lam_params, sub_norm, lambda_init, heads)
    return _out_proj(o, w_out.astype(BF16), h)


def _mixer_b(h, gain, w_in, w_out, forget_bias, q_norm, k_norm):
    d = h.shape[1]
    heads = d // HEAD
    width = heads * HEAD
    colscale = jnp.concatenate([jnp.tile(q_norm, heads) * (LOG2_E * HEAD ** -0.5), jnp.tile(k_norm, heads),
                                jnp.ones((2 * width,), F32)])
    qkvg = _norm_proj(h, gain, w_in[:, :4 * width].astype(BF16), colscale=colscale, norm_cols=2 * width)
    f_logit = _norm_proj(h, gain, _pad_cols(w_in[:, 4 * width:], LANES).astype(BF16), out_dtype=F32)
    bias = jnp.pad(forget_bias, (0, LANES - heads)).reshape(1, LANES)
    cum_t = _fox_cum_log_forget(f_logit, bias)[:heads].reshape(heads, 1, -1)
    o = _fox_attention(qkvg, cum_t, heads)
    return _out_proj(o, w_out.astype(BF16), h)


def _mixer_c(h, gain, w_in, w_out, conv_w, a_log, dt_bias, out_norm):
    d = h.shape[1]
    qk_heads = d // HEAD
    v_heads = 2 * qk_heads
    qk_w = qk_heads * HEAD
    v_w = v_heads * HEAD
    main = 2 * qk_w + 2 * v_w
    proj = _norm_proj(h, gain, w_in[:, :main].astype(BF16))
    ba = _norm_proj(h, gain, _pad_cols(w_in[:, main:], LANES).astype(BF16), out_dtype=F32)
    gates = _gdn_gates(ba, a_log, dt_bias, v_heads)
    qkv = _gdn_conv(proj, conv_w, qk_cols=2 * qk_w, q_cols=qk_w)
    o = _gdn_delta(qkv, proj, gates, out_norm, qk_heads, v_heads, z_col0=2 * qk_w + v_w)
    return _out_proj(o, w_out.astype(BF16), h)


def kernel(x, mix_norm, ffn_norm, final_norm, a_w_in, a_w_out, a_lam_q1, a_lam_k1, a_lam_q2, a_lam_k2,
           a_sub_norm, b_w_in, b_w_out, b_forget_bias, b_q_norm, b_k_norm, c_w_in, c_w_out, c_conv_w,
           c_a_log, c_dt_bias, c_out_norm, ffn_w_gate, ffn_w_up, ffn_w_down):
    batch, seq, d = x.shape
    depth = mix_norm.shape[0]
    outs = []
    for b in range(batch):
        h = x[b]
        for i in range(depth):
            slot = i // N_MIXERS
            if i % N_MIXERS == 0:
                lambda_init = 0.8 - 0.6 * math.exp(-0.3 * i)
                h = _mixer_a(h, mix_norm[i], a_w_in[slot], a_w_out[slot], a_lam_q1[slot], a_lam_k1[slot],
                             a_lam_q2[slot], a_lam_k2[slot], a_sub_norm[slot], lambda_init)
            elif i % N_MIXERS == 1:
                h = _mixer_b(h, mix_norm[i], b_w_in[slot], b_w_out[slot], b_forget_bias[slot],
                             b_q_norm[slot], b_k_norm[slot])
            else:
                h = _mixer_c(h, mix_norm[i], c_w_in[slot], c_w_out[slot], c_conv_w[slot], c_a_log[slot],
                             c_dt_bias[slot], c_out_norm[slot])
            h = _ffn(h, ffn_norm[i], ffn_w_gate[i].astype(BF16), ffn_w_up[i].astype(BF16),
                     ffn_w_down[i].astype(BF16), final_gain=final_norm if i == depth - 1 else None)
        outs.append(h)
    return jnp.stack(outs)
```

```python
import functools
import math

import jax
import jax.numpy as jnp
from jax import lax
from jax.experimental import pallas as pl
from jax.experimental.pallas import tpu as pltpu

F32 = jnp.float32
BF16 = jnp.bfloat16
I32 = jnp.int32

EPS = 1e-6
MASK_CHUNK = 64
HEAD = 128
LANES = 128
N_MIXERS = 3
LOG2_E = math.log2(math.e)
DELTA_CHUNK = 128
VMEM_LIMIT_BYTES = 52 * 1024 * 1024


def _cparams(*sem):
    return pltpu.CompilerParams(dimension_semantics=sem, vmem_limit_bytes=VMEM_LIMIT_BYTES)


def _dot(a, b):
    return jnp.dot(a, b, preferred_element_type=F32)


def _dot_nt(a, b):
    return lax.dot_general(a, b, (((1,), (1,)), ((), ())), preferred_element_type=F32)


def _rms_rows(x):
    return x * lax.rsqrt(jnp.mean(x * x, axis=-1, keepdims=True) + EPS)


def _proj_kernel(*refs, tn, norm_blocks, has_colscale):
    if has_colscale:
        x_ref, g_ref, w_ref, cs_ref, o_ref, xn_ref = refs
    else:
        x_ref, g_ref, w_ref, o_ref, xn_ref = refs
        cs_ref = None
    n = pl.program_id(1)

    @pl.when(n == 0)
    def _():
        xn_ref[...] = (_rms_rows(x_ref[...]) * g_ref[...]).astype(BF16)

    acc = _dot(xn_ref[...], w_ref[...])

    def finish(a):
        if cs_ref is not None:
            a = a * cs_ref[...]
        o_ref[...] = a.astype(o_ref.dtype)

    if norm_blocks == 0:
        finish(acc)
    else:
        @pl.when(n < norm_blocks)
        def _():
            parts = [_rms_rows(acc[:, c * HEAD:(c + 1) * HEAD]) for c in range(tn // HEAD)]
            finish(jnp.concatenate(parts, axis=1))

        @pl.when(n >= norm_blocks)
        def _():
            finish(acc)


def _norm_proj(h, gain, w, n_cols=None, colscale=None, norm_cols=0, out_dtype=BF16, tm=1024, tn=1024):
    s, d = h.shape
    n = w.shape[1] if n_cols is None else n_cols
    tn = min(tn, n)
    assert s % tm == 0 and n % tn == 0 and norm_cols % tn == 0 and tn % HEAD == 0
    in_specs = [
        pl.BlockSpec((tm, d), lambda m, j: (m, 0)),
        pl.BlockSpec((1, d), lambda m, j: (0, 0)),
        pl.BlockSpec((d, tn), lambda m, j: (0, j)),
    ]
    args = [h, gain.reshape(1, d), w]
    if colscale is not None:
        in_specs.append(pl.BlockSpec((1, tn), lambda m, j: (0, j)))
        args.append(colscale.reshape(1, n))
    return pl.pallas_call(
        functools.partial(_proj_kernel, tn=tn, norm_blocks=norm_cols // tn,
                          has_colscale=colscale is not None),
        grid=(s // tm, n // tn),
        in_specs=in_specs,
        out_specs=pl.BlockSpec((tm, tn), lambda m, j: (m, j)),
        out_shape=jax.ShapeDtypeStruct((s, n), out_dtype),
        scratch_shapes=[pltpu.VMEM((tm, d), BF16)],
        compiler_params=_cparams("parallel", "arbitrary"),
        name="norm_proj",
    )(*args)


def _out_proj_kernel(o_ref, w_ref, h_ref, y_ref):
    y_ref[...] = h_ref[...] + _dot(o_ref[...], w_ref[...])


def _out_proj(o, w, h, tm=1024, tn=None):
    s, k = o.shape
    d = w.shape[1]
    if tn is None:
        tn = 1024 if k <= 2048 else 512
    return pl.pallas_call(
        _out_proj_kernel,
        grid=(s // tm, d // tn),
        in_specs=[
            pl.BlockSpec((tm, k), lambda m, j: (m, 0)),
            pl.BlockSpec((k, tn), lambda m, j: (0, j)),
            pl.BlockSpec((tm, tn), lambda m, j: (m, j)),
        ],
        out_specs=pl.BlockSpec((tm, tn), lambda m, j: (m, j)),
        out_shape=jax.ShapeDtypeStruct((s, d), F32),
        compiler_params=_cparams("parallel", "arbitrary"),
        name="out_proj",
    )(o, w, h)


def _ffn_kernel(*refs, has_final):
    if has_final:
        h_ref, g_ref, wg_ref, wu_ref, wd_ref, fg_ref, y_ref, xn_ref = refs
    else:
        h_ref, g_ref, wg_ref, wu_ref, wd_ref, y_ref, xn_ref = refs
        fg_ref = None
    f = pl.program_id(1)

    @pl.when(f == 0)
    def _():
        h = h_ref[...]
        xn_ref[...] = (_rms_rows(h) * g_ref[...]).astype(BF16)
        y_ref[...] = h

    xn = xn_ref[...]
    gate = _dot(xn, wg_ref[...])
    up = _dot(xn, wu_ref[...])
    act = (gate * jax.nn.sigmoid(gate) * up).astype(BF16)
    y_ref[...] += _dot(act, wd_ref[...])

    if fg_ref is not None:
        @pl.when(f == pl.num_programs(1) - 1)
        def _():
            y_ref[...] = _rms_rows(y_ref[...]) * fg_ref[...]


def _ffn(h, gain, w_gate, w_up, w_down, final_gain=None, tm=512, tf=512):
    s, d = h.shape
    dff = w_gate.shape[1]
    in_specs = [
        pl.BlockSpec((tm, d), lambda m, f: (m, 0)),
        pl.BlockSpec((1, d), lambda m, f: (0, 0)),
        pl.BlockSpec((d, tf), lambda m, f: (0, f)),
        pl.BlockSpec((d, tf), lambda m, f: (0, f)),
        pl.BlockSpec((tf, d), lambda m, f: (f, 0)),
    ]
    args = [h, gain.reshape(1, d), w_gate, w_up, w_down]
    if final_gain is not None:
        in_specs.append(pl.BlockSpec((1, d), lambda m, f: (0, 0)))
        args.append(final_gain.reshape(1, d))
    return pl.pallas_call(
        functools.partial(_ffn_kernel, has_final=final_gain is not None),
        grid=(s // tm, dff // tf),
        in_specs=in_specs,
        out_specs=pl.BlockSpec((tm, d), lambda m, f: (m, 0)),
        out_shape=jax.ShapeDtypeStruct((s, d), F32),
        scratch_shapes=[pltpu.VMEM((tm, d), BF16)],
        compiler_params=_cparams("parallel", "arbitrary"),
        name="swiglu",
    )(*args)


def _kv_block_start(t, q0, tk):
    return pl.multiple_of(jnp.where(t < 2, q0 + t * tk, (t - 2) * tk), tk)


def _run_kv_pipeline(n_full_pairs, scores, softmax, weighted_values):
    scores(0, 0, True)
    scores(1, 1, True)
    softmax(0)

    def body(u, carry):
        t = 2 * u
        scores(t + 2, 0, False)
        softmax(1)
        weighted_values(t, 0)
        scores(t + 3, 1, False)
        softmax(0)
        weighted_values(t + 1, 1)
        return carry

    lax.fori_loop(0, n_full_pairs, body, 0)
    n = 2 + 2 * n_full_pairs
    softmax(1)
    weighted_values(n - 2, 0)
    weighted_values(n - 1, 1)


def _diff_attn_kernel(q1_ref, q2_ref, k1_ref, k2_ref, v_ref, lam_ref, sn_ref, o_ref,
                      s_ref, p_ref, a_ref, m_ref, l_ref, acc_ref, *, tq, tk, lambda_init):
    i = pl.program_id(1)
    q0 = pl.multiple_of(i * tq, tq)
    lane_tiles = tk // LANES
    v_tiles = v_ref.shape[1] // LANES
    m_ref[...] = jnp.full(m_ref.shape, -jnp.inf, F32)
    l_ref[...] = jnp.zeros(l_ref.shape, F32)
    acc_ref[...] = jnp.zeros(acc_ref.shape, F32)
    qs = (q1_ref[...], q2_ref[...])
    ks = (k1_ref, k2_ref)

    def scores(t, slot, masked):
        k0 = _kv_block_start(t, q0, tk)
        if masked:
            row = lax.broadcasted_iota(I32, (tq, tk), 0)
            col = lax.broadcasted_iota(I32, (tq, tk), 1) + (k0 - q0)
            visible = (col // MASK_CHUNK) <= (row // MASK_CHUNK)
        for mp in range(2):
            s = _dot_nt(qs[mp], ks[mp][pl.ds(k0, tk), :])
            if masked:
                s = jnp.where(visible, s, -jnp.inf)
            s_ref[slot, mp] = s

    def softmax(slot):
        for mp in range(2):
            s = s_ref[slot, mp]
            m_prev = m_ref[mp]
            m_new = jnp.maximum(m_prev, jnp.max(s, axis=-1, keepdims=True))
            alpha = jnp.exp2(m_prev - m_new)
            l_new = alpha * l_ref[mp]
            for c in range(lane_tiles):
                p = jnp.exp2(s[:, c * LANES:(c + 1) * LANES] - m_new)
                l_new = l_new + p
                p_ref[slot, mp, :, c * LANES:(c + 1) * LANES] = p.astype(BF16)
            l_ref[mp] = l_new
            m_ref[mp] = m_new
            a_ref[slot, mp] = alpha

    def weighted_values(t, slot):
        k0 = _kv_block_start(t, q0, tk)
        v = v_ref[pl.ds(k0, tk), :]
        for mp in range(2):
            alpha = jnp.concatenate([a_ref[slot, mp]] * v_tiles, axis=1)
            acc_ref[mp] = alpha * acc_ref[mp] + _dot(p_ref[slot, mp], v)

    _run_kv_pipeline(i * (tq // (2 * tk)), scores, softmax, weighted_values)

    lam = (jnp.exp(jnp.sum(lam_ref[0:1, :] * lam_ref[1:2, :], keepdims=True))
           - jnp.exp(jnp.sum(lam_ref[2:3, :] * lam_ref[3:4, :], keepdims=True)) + lambda_init)
    l1 = jnp.sum(l_ref[0], axis=-1, keepdims=True)
    l2 = jnp.sum(l_ref[1], axis=-1, keepdims=True)
    o = acc_ref[0] / l1 - lam * (acc_ref[1] / l2)
    o = _rms_rows(o) * sn_ref[...] * (1.0 - lambda_init)
    o_ref[...] = o.astype(o_ref.dtype)


def _diff_attention(qkv, lam_params, sub_norm, lambda_init, heads, tq=512):
    s = qkv.shape[0]
    vd = 2 * HEAD
    tk = tq // 2
    hb = heads
    return pl.pallas_call(
        functools.partial(_diff_attn_kernel, tq=tq, tk=tk, lambda_init=lambda_init),
        grid=(heads, s // tq),
        in_specs=[
            pl.BlockSpec((tq, HEAD), lambda h, i: (i, h)),
            pl.BlockSpec((tq, HEAD), lambda h, i: (i, hb + h)),
            pl.BlockSpec((s, HEAD), lambda h, i: (0, 2 * hb + h)),
            pl.BlockSpec((s, HEAD), lambda h, i: (0, 3 * hb + h)),
            pl.BlockSpec((s, vd), lambda h, i: (0, 2 * hb + h)),
            pl.BlockSpec((4, HEAD), lambda h, i: (0, 0)),
            pl.BlockSpec((1, vd), lambda h, i: (0, 0)),
        ],
        out_specs=pl.BlockSpec((tq, vd), lambda h, i: (i, h)),
        out_shape=jax.ShapeDtypeStruct((s, heads * vd), BF16),
        scratch_shapes=[pltpu.VMEM((2, 2, tq, tk), F32), pltpu.VMEM((2, 2, tq, tk), BF16),
                        pltpu.VMEM((2, 2, tq, LANES), F32), pltpu.VMEM((2, tq, LANES), F32),
                        pltpu.VMEM((2, tq, LANES), F32), pltpu.VMEM((2, tq, vd), F32)],
        compiler_params=_cparams("parallel", "arbitrary"),
        name="diff_attention",
    )(qkv, qkv, qkv, qkv, qkv, lam_params, sub_norm.reshape(1, vd))


def _fox_gate_kernel(fl_ref, b_ref, o_ref, carry_ref, *, tb):
    @pl.when(pl.program_id(0) == 0)
    def _():
        carry_ref[...] = jnp.zeros(carry_ref.shape, F32)

    log_f = jax.nn.log_sigmoid(fl_ref[...] + b_ref[...])
    tri = (lax.broadcasted_iota(I32, (tb, tb), 1) <= lax.broadcasted_iota(I32, (tb, tb), 0)).astype(F32)
    cum = jnp.dot(tri, log_f, preferred_element_type=F32,
                  precision=lax.Precision.HIGHEST) + carry_ref[...]
    carry_ref[...] = cum[tb - 1:tb, :]
    o_ref[...] = cum.T


def _fox_cum_log_forget(f_logit, bias, tb=256):
    s = f_logit.shape[0]
    return pl.pallas_call(
        functools.partial(_fox_gate_kernel, tb=tb),
        grid=(s // tb,),
        in_specs=[pl.BlockSpec((tb, LANES), lambda i: (i, 0)),
                  pl.BlockSpec((1, LANES), lambda i: (0, 0))],
        out_specs=pl.BlockSpec((LANES, tb), lambda i: (0, i)),
        out_shape=jax.ShapeDtypeStruct((LANES, s), F32),
        scratch_shapes=[pltpu.VMEM((1, LANES), F32)],
        compiler_params=_cparams("arbitrary"),
        name="fox_cum_log_forget",
    )(f_logit, bias)


def _fox_attn_kernel(q_ref, k_ref, v_ref, gate_ref, ck_ref, o_ref,
                     s_ref, p_ref, a_ref, m_ref, l_ref, acc_ref, *, tq, tk):
    i = pl.program_id(1)
    q0 = pl.multiple_of(i * tq, tq)
    lane_tiles = tk // LANES
    m_ref[...] = jnp.full(m_ref.shape, -jnp.inf, F32)
    l_ref[...] = jnp.zeros(l_ref.shape, F32)
    acc_ref[...] = jnp.zeros(acc_ref.shape, F32)
    q = q_ref[...]
    base = ck_ref[:, pl.ds(q0, LANES)][:, 0:1]

    def scores(t, slot, masked):
        k0 = _kv_block_start(t, q0, tk)
        s = _dot_nt(q, k_ref[pl.ds(k0, tk), :]) + (base - ck_ref[:, pl.ds(k0, tk)]) * LOG2_E
        if masked:
            row = lax.broadcasted_iota(I32, (tq, tk), 0)
            col = lax.broadcasted_iota(I32, (tq, tk), 1)
            s = jnp.where(col + (k0 - q0) <= row, s, -jnp.inf)
        s_ref[slot] = s

    def softmax(slot):
        s = s_ref[slot]
        m_prev = m_ref[...]
        m_new = jnp.maximum(m_prev, jnp.max(s, axis=-1, keepdims=True))
        alpha = jnp.exp2(m_prev - m_new)
        l_new = alpha * l_ref[...]
        for c in range(lane_tiles):
            p = jnp.exp2(s[:, c * LANES:(c + 1) * LANES] - m_new)
            l_new = l_new + p
            p_ref[slot, :, c * LANES:(c + 1) * LANES] = p.astype(BF16)
        l_ref[...] = l_new
        m_ref[...] = m_new
        a_ref[slot] = alpha

    def weighted_values(t, slot):
        k0 = _kv_block_start(t, q0, tk)
        acc_ref[...] = a_ref[slot] * acc_ref[...] + _dot(p_ref[slot], v_ref[pl.ds(k0, tk), :])

    _run_kv_pipeline(i * (tq // (2 * tk)), scores, softmax, weighted_values)

    l = jnp.sum(l_ref[...], axis=-1, keepdims=True)
    o = acc_ref[...] / l * jax.nn.sigmoid(gate_ref[...].astype(F32))
    o_ref[...] = o.astype(o_ref.dtype)


def _fox_attention(qkvg, cum_t, heads, tq=1024):
    s = qkvg.shape[0]
    tk = tq // 2
    return pl.pallas_call(
        functools.partial(_fox_attn_kernel, tq=tq, tk=tk),
        grid=(heads, s // tq),
        in_specs=[
            pl.BlockSpec((tq, HEAD), lambda h, i: (i, h)),
            pl.BlockSpec((s, HEAD), lambda h, i: (0, heads + h)),
            pl.BlockSpec((s, HEAD), lambda h, i: (0, 2 * heads + h)),
            pl.BlockSpec((tq, HEAD), lambda h, i: (i, 3 * heads + h)),
            pl.BlockSpec((None, 1, s), lambda h, i: (h, 0, 0)),
        ],
        out_specs=pl.BlockSpec((tq, HEAD), lambda h, i: (i, h)),
        out_shape=jax.ShapeDtypeStruct((s, heads * HEAD), BF16),
        scratch_shapes=[pltpu.VMEM((2, tq, tk), F32), pltpu.VMEM((2, tq, tk), BF16),
                        pltpu.VMEM((2, tq, LANES), F32), pltpu.VMEM((tq, LANES), F32),
                        pltpu.VMEM((tq, LANES), F32), pltpu.VMEM((tq, HEAD), F32)],
        compiler_params=_cparams("parallel", "arbitrary"),
        name="fox_attention",
    )(qkvg, qkvg, qkvg, qkvg, cum_t)


def _gdn_conv_kernel(x_ref, halo_ref, w_ref, o_ref, *, tn, qk_blocks, q_blocks, halo_rows):
    m = pl.program_id(0)
    n = pl.program_id(1)
    x = x_ref[...].astype(F32)
    halo = jnp.where(m > 0, halo_ref[...].astype(F32), 0.0)
    w = w_ref[...]
    taps = w.shape[0]
    x_top = x[0:8, :]
    row8 = lax.broadcasted_iota(I32, (8, tn), 0)
    y = x * w[taps - 1:taps, :]
    y_top = x_top * w[taps - 1:taps, :]
    for sh in range(1, taps):
        wj = w[taps - 1 - sh:taps - sh, :]
        y = y + pltpu.roll(x, sh, 0) * wj
        before = pltpu.roll(halo, sh, 0)[0:8, :]
        y_top = y_top + jnp.where(row8 < sh, before, pltpu.roll(x_top, sh, 0)) * wj

    def emit(transform):
        for rows, a in ((slice(None), y), (slice(0, 8), y_top)):
            o_ref[rows, :] = transform(a * jax.nn.sigmoid(a)).astype(o_ref.dtype)

    def l2_heads(scale):
        def transform(a):
            parts = []
            for c in range(tn // HEAD):
                b = a[:, c * HEAD:(c + 1) * HEAD]
                parts.append(b * (lax.rsqrt(jnp.sum(b * b, axis=-1, keepdims=True) + EPS) * scale))
            return jnp.concatenate(parts, axis=1)
        return transform

    @pl.when(n < q_blocks)
    def _():
        emit(l2_heads(HEAD ** -0.5))

    @pl.when((n >= q_blocks) & (n < qk_blocks))
    def _():
        emit(l2_heads(1.0))

    @pl.when(n >= qk_blocks)
    def _():
        emit(lambda a: a)


def _gdn_conv(proj, conv_w, qk_cols, q_cols, tm=512, tn=512):
    s = proj.shape[0]
    taps, c = conv_w.shape
    halo_rows = 16
    assert tm % halo_rows == 0 and c % tn == 0 and qk_cols % tn == 0 and q_cols % tn == 0
    return pl.pallas_call(
        functools.partial(_gdn_conv_kernel, tn=tn, qk_blocks=qk_cols // tn, q_blocks=q_cols // tn,
                          halo_rows=halo_rows),
        grid=(s // tm, c // tn),
        in_specs=[
            pl.BlockSpec((tm, tn), lambda m, j: (m, j)),
            pl.BlockSpec((halo_rows, tn), lambda m, j: (jnp.maximum(m * (tm // halo_rows) - 1, 0), j)),
            pl.BlockSpec((taps, tn), lambda m, j: (0, j)),
        ],
        out_specs=pl.BlockSpec((tm, tn), lambda m, j: (m, j)),
        out_shape=jax.ShapeDtypeStruct((s, c), BF16),
        compiler_params=_cparams("parallel", "arbitrary"),
        name="gdn_conv",
    )(proj, proj, conv_w)


def _gdn_gate_kernel(x_ref, alog_ref, dtb_ref, o_ref, *, v_heads):
    x = x_ref[...]
    beta = jax.nn.sigmoid(x)
    g = -jnp.exp(alog_ref[...]) * jax.nn.softplus(x + dtb_ref[...])
    lane = lax.broadcasted_iota(I32, x.shape, 1)
    o_ref[...] = jnp.where(lane < v_heads, beta, g).T


def _gdn_gates(ba, a_log, dt_bias, v_heads, tb=512):
    s = ba.shape[0]
    pad = LANES - 2 * v_heads
    alog = jnp.concatenate([jnp.zeros((v_heads,), F32), a_log, jnp.zeros((pad,), F32)]).reshape(1, LANES)
    dtb = jnp.concatenate([jnp.zeros((v_heads,), F32), dt_bias, jnp.zeros((pad,), F32)]).reshape(1, LANES)
    return pl.pallas_call(
        functools.partial(_gdn_gate_kernel, v_heads=v_heads),
        grid=(s // tb,),
        in_specs=[pl.BlockSpec((tb, LANES), lambda i: (i, 0)),
                  pl.BlockSpec((1, LANES), lambda i: (0, 0)),
                  pl.BlockSpec((1, LANES), lambda i: (0, 0))],
        out_specs=pl.BlockSpec((LANES, tb), lambda i: (0, i)),
        out_shape=jax.ShapeDtypeStruct((LANES, s), F32),
        compiler_params=_cparams("parallel"),
        name="gdn_gates",
    )(ba, alog, dtb)


def _unit_lower_inverses(lows, ii, jj):
    c = lows[0].shape[0]
    xor = ii ^ jj
    eye = (ii == jj).astype(F32)
    xs = [eye - jnp.where(xor < 2, low, 0.0) for low in lows]
    b = 2
    while b < c:
        lower_left = (xor >= b) & (xor < 2 * b)
        offs = [jnp.where(lower_left, low, 0.0).astype(BF16) for low in lows]
        xbs = [x.astype(BF16) for x in xs]
        ys = [_dot(xb, off).astype(BF16) for xb, off in zip(xbs, offs)]
        xs = [x - _dot(y, xb) for x, y, xb in zip(xs, ys, xbs)]
        b *= 2
    return xs


def _gdn_delta_kernel(q_ref, k_ref, v_ref, z_ref, beta_ref, g_ref, gain_ref, o_ref,
                      state_ref, u_ref, wq_ref, intra_ref, kdt_ref, *, n_chunks, k_heads, rep, par):
    c_len = DELTA_CHUNK

    @pl.when(pl.program_id(1) == 0)
    def _():
        state_ref[...] = jnp.zeros(state_ref.shape, F32)

    def chunk_local(kq, cg):
        ii = lax.broadcasted_iota(I32, (c_len, c_len), 0)
        jj = lax.broadcasted_iota(I32, (c_len, c_len), 1)
        causal = jj <= ii
        strict = jj < ii
        diag = jj == ii
        qk_cols = slice(kq * HEAD, (kq + 1) * HEAD)
        cs = [cg * par + cc for cc in range(par)]
        rows = [pl.ds(pl.multiple_of(c * c_len, c_len), c_len) for c in cs]
        ks = [k_ref[r, qk_cols] for r in rows]
        qs = [q_ref[r, qk_cols] for r in rows]
        kks = [_dot_nt(k, k) for k in ks]
        qks = [_dot_nt(q, k) for q, k in zip(qs, ks)]
        kfs = [k.astype(F32) for k in ks]
        kts = [kf.T for kf in kfs]
        chains = [(cc, kq * rep + e) for cc in range(par) for e in range(rep)]
        g_rows = [g_ref[hv, cs[cc]] for cc, hv in chains]
        b_rows = [beta_ref[hv, cs[cc]] for cc, hv in chains]
        gc_cols = [jnp.sum(jnp.where(causal, g, 0.0), axis=1, keepdims=True) for g in g_rows]
        gc_rows = [jnp.sum(jnp.where(diag, gc, 0.0), axis=0, keepdims=True) for gc in gc_cols]
        b_cols = [jnp.sum(jnp.where(diag, b, 0.0), axis=1, keepdims=True) for b in b_rows]
        g_lasts = [jnp.sum(g, axis=1, keepdims=True) for g in g_rows]
        decays = [jnp.where(causal, jnp.exp(gc - gr), 0.0) for gc, gr in zip(gc_cols, gc_rows)]
        lows = [jnp.where(strict, bc * kks[cc] * dec, 0.0)
                for (cc, hv), bc, dec in zip(chains, b_cols, decays)]
        t_invs = _unit_lower_inverses(lows, ii, jj)
        for n, (cc, hv) in enumerate(chains):
            c = cs[cc]
            e_gc = jnp.exp(gc_cols[n])
            v = v_ref[rows[cc], hv * HEAD:(hv + 1) * HEAD].astype(F32)
            kb = kfs[cc] * b_cols[n]
            rhs = jnp.concatenate([v * b_cols[n], kb * e_gc], axis=1).astype(BF16)
            sol = _dot(t_invs[n].astype(BF16), rhs)
            u_ref[c, hv] = sol[:, :HEAD]
            wq_ref[c, hv, 0:c_len, :] = sol[:, HEAD:].astype(BF16)
            wq_ref[c, hv, c_len:2 * c_len, :] = (qs[cc].astype(F32) * e_gc).astype(BF16)
            intra_ref[c, hv] = (qks[cc] * decays[n]).astype(BF16)
            kdt_ref[c, hv] = (kts[cc] * jnp.exp(g_lasts[n] - gc_rows[n])).astype(BF16)

    for kq in range(k_heads):
        def local_body(cg, carry, kq=kq):
            chunk_local(kq, cg)
            return carry
        lax.fori_loop(0, n_chunks // par, local_body, 0)

    gain = gain_ref[...]
    heads = range(k_heads * rep)

    def chunk_recurrent(c, carry):
        rows = pl.ds(pl.multiple_of(c * c_len, c_len), c_len)
        states = [state_ref[hv] for hv in heads]
        ws_qs = [_dot(wq_ref[c, hv], states[hv].astype(BF16)) for hv in heads]
        v_news = [(u_ref[c, hv] - ws_qs[hv][:c_len]).astype(BF16) for hv in heads]
        upds = [_dot(kdt_ref[c, hv], v_news[hv]) for hv in heads]
        for hv in heads:
            decay_all = jnp.exp(jnp.sum(g_ref[hv, c], axis=1, keepdims=True))
            state_ref[hv] = states[hv] * decay_all + upds[hv]
        for hv in heads:
            o = ws_qs[hv][c_len:] + _dot(intra_ref[c, hv], v_news[hv])
            z = z_ref[rows, hv * HEAD:(hv + 1) * HEAD].astype(F32)
            o = _rms_rows(o) * gain * (z * jax.nn.sigmoid(z))
            o_ref[rows, hv * HEAD:(hv + 1) * HEAD] = o.astype(o_ref.dtype)
        return carry

    lax.fori_loop(0, n_chunks, chunk_recurrent, 0)


def _gdn_delta(qkv, proj, gates, out_gain, qk_heads, v_heads, z_col0, tt=1024, par=8, k_heads=2):
    s = qkv.shape[0]
    rep = v_heads // qk_heads
    c_len = DELTA_CHUNK
    n_chunks = tt // c_len
    g4 = gates.reshape(LANES, s // c_len, 1, c_len)
    qw = k_heads * HEAD
    vw = k_heads * rep * HEAD
    nh = k_heads * rep
    return pl.pallas_call(
        functools.partial(_gdn_delta_kernel, n_chunks=n_chunks, k_heads=k_heads, rep=rep, par=par),
        grid=(qk_heads // k_heads, s // tt),
        in_specs=[
            pl.BlockSpec((tt, qw), lambda h, t: (t, h)),
            pl.BlockSpec((tt, qw), lambda h, t: (t, qk_heads // k_heads + h)),
            pl.BlockSpec((tt, vw), lambda h, t: (t, (2 * qk_heads * HEAD) // vw + h)),
            pl.BlockSpec((tt, vw), lambda h, t: (t, z_col0 // vw + h)),
            pl.BlockSpec((nh, n_chunks, 1, c_len), lambda h, t: (h, t, 0, 0)),
            pl.BlockSpec((nh, n_chunks, 1, c_len), lambda h, t: (v_heads // nh + h, t, 0, 0)),
            pl.BlockSpec((1, HEAD), lambda h, t: (0, 0)),
        ],
        out_specs=pl.BlockSpec((tt, vw), lambda h, t: (t, h)),
        out_shape=jax.ShapeDtypeStruct((s, v_heads * HEAD), BF16),
        scratch_shapes=[pltpu.VMEM((nh, HEAD, HEAD), F32),
                        pltpu.VMEM((n_chunks, nh, c_len, HEAD), F32),
                        pltpu.VMEM((n_chunks, nh, 2 * c_len, HEAD), BF16),
                        pltpu.VMEM((n_chunks, nh, c_len, c_len), BF16),
                        pltpu.VMEM((n_chunks, nh, HEAD, c_len), BF16)],
        compiler_params=_cparams("parallel", "arbitrary"),
        name="gdn_delta",
    )(qkv, qkv, qkv, proj, g4, g4, out_gain.reshape(1, HEAD))


def _pad_cols(w, width):
    return jnp.pad(w, ((0, 0), (0, width - w.shape[1])))


def _mixer_a(h, gain, w_in, w_out, lam_q1, lam_k1, lam_q2, lam_k2, sub_norm, lambda_init):
    d = h.shape[1]
    heads = d // (2 * HEAD)
    qk_cols = 2 * heads * HEAD
    colscale = jnp.concatenate([jnp.full((qk_cols,), LOG2_E * HEAD ** -0.5, F32),
                                jnp.ones((w_in.shape[1] - qk_cols,), F32)])
    qkv = _norm_proj(h, gain, w_in.astype(BF16), colscale=colscale)
    lam_params = jnp.stack([lam_q1, lam_k1, lam_q2, lam_k2]).astype(F32)
    o = _diff_attention(qkv, lam_params, sub_norm, lambda_init, heads)
    return _out_proj(o, w_out.astype(BF16), h)


def _mixer_b(h, gain, w_in, w_out, forget_bias, q_norm, k_norm):
    d = h.shape[1]
    heads = d // HEAD
    width = heads * HEAD
    colscale = jnp.concatenate([jnp.tile(q_norm, heads) * (LOG2_E * HEAD ** -0.5), jnp.tile(k_norm, heads),
                                jnp.ones((2 * width,), F32)])
    qkvg = _norm_proj(h, gain, w_in.astype(BF16), n_cols=4 * width, colscale=colscale, norm_cols=2 * width)
    f_logit = _norm_proj(h, gain, _pad_cols(w_in[:, 4 * width:], LANES).astype(BF16), out_dtype=F32)
    bias = jnp.pad(forget_bias, (0, LANES - heads)).reshape(1, LANES)
    cum_t = _fox_cum_log_forget(f_logit, bias)[:heads].reshape(heads, 1, -1)
    o = _fox_attention(qkvg, cum_t, heads)
    return _out_proj(o, w_out.astype(BF16), h)


def _mixer_c(h, gain, w_in, w_out, conv_w, a_log, dt_bias, out_norm):
    d = h.shape[1]
    qk_heads = d // HEAD
    v_heads = 2 * qk_heads
    qk_w = qk_heads * HEAD
    v_w = v_heads * HEAD
    main = 2 * qk_w + 2 * v_w
    proj = _norm_proj(h, gain, w_in.astype(BF16), n_cols=main)
    ba = _norm_proj(h, gain, _pad_cols(w_in[:, main:], LANES).astype(BF16), out_dtype=F32)
    gates = _gdn_gates(ba, a_log, dt_bias, v_heads)
    qkv = _gdn_conv(proj, conv_w, qk_cols=2 * qk_w, q_cols=qk_w)
    o = _gdn_delta(qkv, proj, gates, out_norm, qk_heads, v_heads, z_col0=2 * qk_w + v_w)
    return _out_proj(o, w_out.astype(BF16), h)


def kernel(x, mix_norm, ffn_norm, final_norm, a_w_in, a_w_out, a_lam_q1, a_lam_k1, a_lam_q2, a_lam_k2,
           a_sub_norm, b_w_in, b_w_out, b_forget_bias, b_q_norm, b_k_norm, c_w_in, c_w_out, c_conv_w,
           c_a_log, c_dt_bias, c_out_norm, ffn_w_gate, ffn_w_up, ffn_w_down):
    batch, seq, d = x.shape
    depth = mix_norm.shape[0]
    outs = []
    for b in range(batch):
        h = x.reshape(seq, d) if batch == 1 else x[b]
        for i in range(depth):
            slot = i // N_MIXERS
            if i % N_MIXERS == 0:
                lambda_init = 0.8 - 0.6 * math.exp(-0.3 * i)
                h = _mixer_a(h, mix_norm[i], a_w_in[slot], a_w_out[slot], a_lam_q1[slot], a_lam_k1[slot],
                             a_lam_q2[slot], a_lam_k2[slot], a_sub_norm[slot], lambda_init)
            elif i % N_MIXERS == 1:
                h = _mixer_b(h, mix_norm[i], b_w_in[slot], b_w_out[slot], b_forget_bias[slot],
                             b_q_norm[slot], b_k_norm[slot])
            else:
                h = _mixer_c(h, mix_norm[i], c_w_in[slot], c_w_out[slot], c_conv_w[slot], c_a_log[slot],
                             c_dt_bias[slot], c_out_norm[slot])
            h = _ffn(h, ffn_norm[i], ffn_w_gate[i].astype(BF16), ffn_w_up[i].astype(BF16),
                     ffn_w_down[i].astype(BF16), final_gain=final_norm if i == depth - 1 else None)
        outs.append(h)
    return outs[0].reshape(1, seq, d) if batch == 1 else jnp.stack(outs)
```

```python
import functools
import math

import jax
import jax.numpy as jnp
from jax import lax
from jax.experimental import pallas as pl
from jax.experimental.pallas import tpu as pltpu

F32 = jnp.float32
BF16 = jnp.bfloat16
I32 = jnp.int32

EPS = 1e-6
MASK_CHUNK = 64
HEAD = 128
LANES = 128
N_MIXERS = 3
LOG2_E = math.log2(math.e)
DELTA_CHUNK = 128
VMEM_LIMIT_BYTES = 52 * 1024 * 1024


def _cparams(*sem):
    return pltpu.CompilerParams(dimension_semantics=sem, vmem_limit_bytes=VMEM_LIMIT_BYTES)


def _dot(a, b):
    return jnp.dot(a, b, preferred_element_type=F32)


def _dot_nt(a, b):
    return lax.dot_general(a, b, (((1,), (1,)), ((), ())), preferred_element_type=F32)


def _rms_rows(x):
    return x * lax.rsqrt(jnp.mean(x * x, axis=-1, keepdims=True) + EPS)


def _proj_kernel(*refs, tn, norm_blocks, has_colscale):
    if has_colscale:
        x_ref, g_ref, w_ref, cs_ref, o_ref, xn_ref = refs
    else:
        x_ref, g_ref, w_ref, o_ref, xn_ref = refs
        cs_ref = None
    n = pl.program_id(1)

    @pl.when(n == 0)
    def _():
        xn_ref[...] = (_rms_rows(x_ref[...]) * g_ref[...]).astype(BF16)

    acc = _dot(xn_ref[...], w_ref[...])

    def finish(a):
        if cs_ref is not None:
            a = a * cs_ref[...]
        o_ref[...] = a.astype(o_ref.dtype)

    if norm_blocks == 0:
        finish(acc)
    else:
        @pl.when(n < norm_blocks)
        def _():
            parts = [_rms_rows(acc[:, c * HEAD:(c + 1) * HEAD]) for c in range(tn // HEAD)]
            finish(jnp.concatenate(parts, axis=1))

        @pl.when(n >= norm_blocks)
        def _():
            finish(acc)


def _norm_proj(h, gain, w, layer, n_cols=None, colscale=None, norm_cols=0, out_dtype=BF16, tm=1024, tn=1024):
    s, d = h.shape
    n = w.shape[2] if n_cols is None else n_cols
    tn = min(tn, n)
    assert s % tm == 0 and n % tn == 0 and norm_cols % tn == 0 and tn % HEAD == 0
    in_specs = [
        pl.BlockSpec((tm, d), lambda m, j: (m, 0)),
        pl.BlockSpec((1, d), lambda m, j: (0, 0)),
        pl.BlockSpec((None, d, tn), lambda m, j: (layer, 0, j)),
    ]
    args = [h, gain.reshape(1, d), w]
    if colscale is not None:
        in_specs.append(pl.BlockSpec((1, tn), lambda m, j: (0, j)))
        args.append(colscale.reshape(1, n))
    return pl.pallas_call(
        functools.partial(_proj_kernel, tn=tn, norm_blocks=norm_cols // tn,
                          has_colscale=colscale is not None),
        grid=(s // tm, n // tn),
        in_specs=in_specs,
        out_specs=pl.BlockSpec((tm, tn), lambda m, j: (m, j)),
        out_shape=jax.ShapeDtypeStruct((s, n), out_dtype),
        scratch_shapes=[pltpu.VMEM((tm, d), BF16)],
        compiler_params=_cparams("parallel", "arbitrary"),
        name="norm_proj",
    )(*args)


def _out_proj_kernel(o_ref, w_ref, h_ref, y_ref):
    y_ref[...] = h_ref[...] + _dot(o_ref[...], w_ref[...])


def _out_proj(o, w, layer, h, tm=1024, tn=None):
    s, k = o.shape
    d = w.shape[2]
    if tn is None:
        tn = 1024 if k <= 2048 else 512
    return pl.pallas_call(
        _out_proj_kernel,
        grid=(s // tm, d // tn),
        in_specs=[
            pl.BlockSpec((tm, k), lambda m, j: (m, 0)),
            pl.BlockSpec((None, k, tn), lambda m, j: (layer, 0, j)),
            pl.BlockSpec((tm, tn), lambda m, j: (m, j)),
        ],
        out_specs=pl.BlockSpec((tm, tn), lambda m, j: (m, j)),
        out_shape=jax.ShapeDtypeStruct((s, d), F32),
        compiler_params=_cparams("parallel", "arbitrary"),
        name="out_proj",
    )(o, w, h)


def _ffn_kernel(*refs, has_final):
    if has_final:
        h_ref, g_ref, wg_ref, wu_ref, wd_ref, fg_ref, y_ref, xn_ref = refs
    else:
        h_ref, g_ref, wg_ref, wu_ref, wd_ref, y_ref, xn_ref = refs
        fg_ref = None
    f = pl.program_id(1)

    @pl.when(f == 0)
    def _():
        h = h_ref[...]
        xn_ref[...] = (_rms_rows(h) * g_ref[...]).astype(BF16)
        y_ref[...] = h

    xn = xn_ref[...]
    gate = _dot(xn, wg_ref[...])
    up = _dot(xn, wu_ref[...])
    act = (gate * jax.nn.sigmoid(gate) * up).astype(BF16)
    y_ref[...] += _dot(act, wd_ref[...])

    if fg_ref is not None:
        @pl.when(f == pl.num_programs(1) - 1)
        def _():
            y_ref[...] = _rms_rows(y_ref[...]) * fg_ref[...]


def _ffn(h, gain, w_gate, w_up, w_down, layer, final_gain=None, tm=512, tf=512):
    s, d = h.shape
    dff = w_gate.shape[2]
    in_specs = [
        pl.BlockSpec((tm, d), lambda m, f: (m, 0)),
        pl.BlockSpec((1, d), lambda m, f: (0, 0)),
        pl.BlockSpec((None, d, tf), lambda m, f: (layer, 0, f)),
        pl.BlockSpec((None, d, tf), lambda m, f: (layer, 0, f)),
        pl.BlockSpec((None, tf, d), lambda m, f: (layer, f, 0)),
    ]
    args = [h, gain.reshape(1, d), w_gate, w_up, w_down]
    if final_gain is not None:
        in_specs.append(pl.BlockSpec((1, d), lambda m, f: (0, 0)))
        args.append(final_gain.reshape(1, d))
    return pl.pallas_call(
        functools.partial(_ffn_kernel, has_final=final_gain is not None),
        grid=(s // tm, dff // tf),
        in_specs=in_specs,
        out_specs=pl.BlockSpec((tm, d), lambda m, f: (m, 0)),
        out_shape=jax.ShapeDtypeStruct((s, d), F32),
        scratch_shapes=[pltpu.VMEM((tm, d), BF16)],
        compiler_params=_cparams("parallel", "arbitrary"),
        name="swiglu",
    )(*args)


def _kv_block_start(t, q0, tk):
    return pl.multiple_of(jnp.where(t < 2, q0 + t * tk, (t - 2) * tk), tk)


def _run_kv_pipeline(n_full_pairs, scores, softmax, weighted_values):
    scores(0, 0, True)
    scores(1, 1, True)
    softmax(0)

    def body(u, carry):
        t = 2 * u
        scores(t + 2, 0, False)
        softmax(1)
        weighted_values(t, 0)
        scores(t + 3, 1, False)
        softmax(0)
        weighted_values(t + 1, 1)
        return carry

    lax.fori_loop(0, n_full_pairs, body, 0)
    n = 2 + 2 * n_full_pairs
    softmax(1)
    weighted_values(n - 2, 0)
    weighted_values(n - 1, 1)


def _diff_attn_kernel(q1_ref, q2_ref, k1_ref, k2_ref, v_ref, lam_ref, sn_ref, o_ref,
                      s_ref, p_ref, a_ref, m_ref, l_ref, acc_ref, *, tq, tk, lambda_init):
    i = pl.program_id(1)
    q0 = pl.multiple_of(i * tq, tq)
    lane_tiles = tk // LANES
    v_tiles = v_ref.shape[1] // LANES
    m_ref[...] = jnp.full(m_ref.shape, -jnp.inf, F32)
    l_ref[...] = jnp.zeros(l_ref.shape, F32)
    acc_ref[...] = jnp.zeros(acc_ref.shape, F32)
    qs = (q1_ref[...], q2_ref[...])
    ks = (k1_ref, k2_ref)

    def scores(t, slot, masked):
        k0 = _kv_block_start(t, q0, tk)
        if masked:
            row = lax.broadcasted_iota(I32, (tq, tk), 0)
            col = lax.broadcasted_iota(I32, (tq, tk), 1) + (k0 - q0)
            visible = (col // MASK_CHUNK) <= (row // MASK_CHUNK)
        for mp in range(2):
            s = _dot_nt(qs[mp], ks[mp][pl.ds(k0, tk), :])
            if masked:
                s = jnp.where(visible, s, -jnp.inf)
            s_ref[slot, mp] = s

    def softmax(slot):
        for mp in range(2):
            s = s_ref[slot, mp]
            m_prev = m_ref[mp]
            m_new = jnp.maximum(m_prev, jnp.max(s, axis=-1, keepdims=True))
            alpha = jnp.exp2(m_prev - m_new)
            l_new = alpha * l_ref[mp]
            for c in range(lane_tiles):
                p = jnp.exp2(s[:, c * LANES:(c + 1) * LANES] - m_new)
                l_new = l_new + p
                p_ref[slot, mp, :, c * LANES:(c + 1) * LANES] = p.astype(BF16)
            l_ref[mp] = l_new
            m_ref[mp] = m_new
            a_ref[slot, mp] = alpha

    def weighted_values(t, slot):
        k0 = _kv_block_start(t, q0, tk)
        v = v_ref[pl.ds(k0, tk), :]
        for mp in range(2):
            alpha = jnp.concatenate([a_ref[slot, mp]] * v_tiles, axis=1)
            acc_ref[mp] = alpha * acc_ref[mp] + _dot(p_ref[slot, mp], v)

    _run_kv_pipeline(i * (tq // (2 * tk)), scores, softmax, weighted_values)

    lam = (jnp.exp(jnp.sum(lam_ref[0:1, :] * lam_ref[1:2, :], keepdims=True))
           - jnp.exp(jnp.sum(lam_ref[2:3, :] * lam_ref[3:4, :], keepdims=True)) + lambda_init)
    l1 = jnp.sum(l_ref[0], axis=-1, keepdims=True)
    l2 = jnp.sum(l_ref[1], axis=-1, keepdims=True)
    o = acc_ref[0] / l1 - lam * (acc_ref[1] / l2)
    o = _rms_rows(o) * sn_ref[...] * (1.0 - lambda_init)
    o_ref[...] = o.astype(o_ref.dtype)


def _diff_attention(qkv, lam_params, sub_norm, lambda_init, heads, tq=512):
    s = qkv.shape[0]
    vd = 2 * HEAD
    tk = tq // 2
    hb = heads
    return pl.pallas_call(
        functools.partial(_diff_attn_kernel, tq=tq, tk=tk, lambda_init=lambda_init),
        grid=(heads, s // tq),
        in_specs=[
            pl.BlockSpec((tq, HEAD), lambda h, i: (i, h)),
            pl.BlockSpec((tq, HEAD), lambda h, i: (i, hb + h)),
            pl.BlockSpec((s, HEAD), lambda h, i: (0, 2 * hb + h)),
            pl.BlockSpec((s, HEAD), lambda h, i: (0, 3 * hb + h)),
            pl.BlockSpec((s, vd), lambda h, i: (0, 2 * hb + h)),
            pl.BlockSpec((4, HEAD), lambda h, i: (0, 0)),
            pl.BlockSpec((1, vd), lambda h, i: (0, 0)),
        ],
        out_specs=pl.BlockSpec((tq, vd), lambda h, i: (i, h)),
        out_shape=jax.ShapeDtypeStruct((s, heads * vd), BF16),
        scratch_shapes=[pltpu.VMEM((2, 2, tq, tk), F32), pltpu.VMEM((2, 2, tq, tk), BF16),
                        pltpu.VMEM((2, 2, tq, LANES), F32), pltpu.VMEM((2, tq, LANES), F32),
                        pltpu.VMEM((2, tq, LANES), F32), pltpu.VMEM((2, tq, vd), F32)],
        compiler_params=_cparams("parallel", "arbitrary"),
        name="diff_attention",
    )(qkv, qkv, qkv, qkv, qkv, lam_params, sub_norm.reshape(1, vd))


def _fox_gate_kernel(fl_ref, b_ref, o_ref, carry_ref, *, tb):
    @pl.when(pl.program_id(0) == 0)
    def _():
        carry_ref[...] = jnp.zeros(carry_ref.shape, F32)

    log_f = jax.nn.log_sigmoid(fl_ref[...] + b_ref[...])
    tri = (lax.broadcasted_iota(I32, (tb, tb), 1) <= lax.broadcasted_iota(I32, (tb, tb), 0)).astype(F32)
    cum = jnp.dot(tri, log_f, preferred_element_type=F32,
                  precision=lax.Precision.HIGHEST) + carry_ref[...]
    carry_ref[...] = cum[tb - 1:tb, :]
    o_ref[...] = cum.T


def _fox_cum_log_forget(f_logit, bias, tb=256):
    s = f_logit.shape[0]
    return pl.pallas_call(
        functools.partial(_fox_gate_kernel, tb=tb),
        grid=(s // tb,),
        in_specs=[pl.BlockSpec((tb, LANES), lambda i: (i, 0)),
                  pl.BlockSpec((1, LANES), lambda i: (0, 0))],
        out_specs=pl.BlockSpec((LANES, tb), lambda i: (0, i)),
        out_shape=jax.ShapeDtypeStruct((LANES, s), F32),
        scratch_shapes=[pltpu.VMEM((1, LANES), F32)],
        compiler_params=_cparams("arbitrary"),
        name="fox_cum_log_forget",
    )(f_logit, bias)


def _fox_attn_kernel(q_ref, k_ref, v_ref, gate_ref, ck_ref, o_ref,
                     s_ref, p_ref, a_ref, m_ref, l_ref, acc_ref, *, tq, tk):
    i = pl.program_id(1)
    q0 = pl.multiple_of(i * tq, tq)
    lane_tiles = tk // LANES
    m_ref[...] = jnp.full(m_ref.shape, -jnp.inf, F32)
    l_ref[...] = jnp.zeros(l_ref.shape, F32)
    acc_ref[...] = jnp.zeros(acc_ref.shape, F32)
    q = q_ref[...]
    base = ck_ref[:, pl.ds(q0, LANES)][:, 0:1]

    def scores(t, slot, masked):
        k0 = _kv_block_start(t, q0, tk)
        s = _dot_nt(q, k_ref[pl.ds(k0, tk), :]) + (base - ck_ref[:, pl.ds(k0, tk)]) * LOG2_E
        if masked:
            row = lax.broadcasted_iota(I32, (tq, tk), 0)
            col = lax.broadcasted_iota(I32, (tq, tk), 1)
            s = jnp.where(col + (k0 - q0) <= row, s, -jnp.inf)
        s_ref[slot] = s

    def softmax(slot):
        s = s_ref[slot]
        m_prev = m_ref[...]
        m_new = jnp.maximum(m_prev, jnp.max(s, axis=-1, keepdims=True))
        alpha = jnp.exp2(m_prev - m_new)
        l_new = alpha * l_ref[...]
        for c in range(lane_tiles):
            p = jnp.exp2(s[:, c * LANES:(c + 1) * LANES] - m_new)
            l_new = l_new + p
            p_ref[slot, :, c * LANES:(c + 1) * LANES] = p.astype(BF16)
        l_ref[...] = l_new
        m_ref[...] = m_new
        a_ref[slot] = alpha

    def weighted_values(t, slot):
        k0 = _kv_block_start(t, q0, tk)
        acc_ref[...] = a_ref[slot] * acc_ref[...] + _dot(p_ref[slot], v_ref[pl.ds(k0, tk), :])

    _run_kv_pipeline(i * (tq // (2 * tk)), scores, softmax, weighted_values)

    l = jnp.sum(l_ref[...], axis=-1, keepdims=True)
    o = acc_ref[...] / l * jax.nn.sigmoid(gate_ref[...].astype(F32))
    o_ref[...] = o.astype(o_ref.dtype)


def _fox_attention(qkvg, cum_t, heads, tq=1024):
    s = qkvg.shape[0]
    tk = tq // 2
    return pl.pallas_call(
        functools.partial(_fox_attn_kernel, tq=tq, tk=tk),
        grid=(heads, s // tq),
        in_specs=[
            pl.BlockSpec((tq, HEAD), lambda h, i: (i, h)),
            pl.BlockSpec((s, HEAD), lambda h, i: (0, heads + h)),
            pl.BlockSpec((s, HEAD), lambda h, i: (0, 2 * heads + h)),
            pl.BlockSpec((tq, HEAD), lambda h, i: (i, 3 * heads + h)),
            pl.BlockSpec((None, 1, s), lambda h, i: (h, 0, 0)),
        ],
        out_specs=pl.BlockSpec((tq, HEAD), lambda h, i: (i, h)),
        out_shape=jax.ShapeDtypeStruct((s, heads * HEAD), BF16),
        scratch_shapes=[pltpu.VMEM((2, tq, tk), F32), pltpu.VMEM((2, tq, tk), BF16),
                        pltpu.VMEM((2, tq, LANES), F32), pltpu.VMEM((tq, LANES), F32),
                        pltpu.VMEM((tq, LANES), F32), pltpu.VMEM((tq, HEAD), F32)],
        compiler_params=_cparams("parallel", "arbitrary"),
        name="fox_attention",
    )(qkvg, qkvg, qkvg, qkvg, cum_t)


def _gdn_proj_kernel(x_ref, g_ref, w_ref, cw_ref, o_ref, xn_ref, halo_ref, *,
                     tn, q_blocks, qk_blocks, conv_blocks):
    m = pl.program_id(0)
    n = pl.program_id(1)

    @pl.when(n == 0)
    def _():
        xn_ref[...] = (_rms_rows(x_ref[...]) * g_ref[...]).astype(BF16)

    acc = _dot(xn_ref[...], w_ref[...])
    tm = acc.shape[0]

    def conv_silu():
        w = cw_ref[...]
        taps = w.shape[0]
        halo = jnp.where(m > 0, halo_ref[n], 0.0)
        halo_ref[n] = acc[tm - 8:tm, :]
        x_top = acc[0:8, :]
        row8 = lax.broadcasted_iota(I32, (8, tn), 0)
        y = acc * w[taps - 1:taps, :]
        y_top = x_top * w[taps - 1:taps, :]
        for sh in range(1, taps):
            wj = w[taps - 1 - sh:taps - sh, :]
            y = y + pltpu.roll(acc, sh, 0) * wj
            before = pltpu.roll(halo, sh, 0)
            y_top = y_top + jnp.where(row8 < sh, before, pltpu.roll(x_top, sh, 0)) * wj
        return y * jax.nn.sigmoid(y), y_top * jax.nn.sigmoid(y_top)

    def emit(transform):
        y, y_top = conv_silu()
        o_ref[...] = transform(y).astype(o_ref.dtype)
        o_ref[0:8, :] = transform(y_top).astype(o_ref.dtype)

    def l2_heads(scale):
        def transform(a):
            parts = []
            for c in range(tn // HEAD):
                b = a[:, c * HEAD:(c + 1) * HEAD]
                parts.append(b * (lax.rsqrt(jnp.sum(b * b, axis=-1, keepdims=True) + EPS) * scale))
            return jnp.concatenate(parts, axis=1)
        return transform

    @pl.when(n < q_blocks)
    def _():
        emit(l2_heads(HEAD ** -0.5))

    @pl.when((n >= q_blocks) & (n < qk_blocks))
    def _():
        emit(l2_heads(1.0))

    @pl.when((n >= qk_blocks) & (n < conv_blocks))
    def _():
        emit(lambda a: a)

    @pl.when(n >= conv_blocks)
    def _():
        o_ref[...] = acc.astype(o_ref.dtype)


def _gdn_norm_proj_conv(h, gain, w, layer, conv_w, n_cols, qk_cols, q_cols, tm=1024, tn=1024):
    s, d = h.shape
    taps, c = conv_w.shape
    assert s % tm == 0 and n_cols % tn == 0 and c % tn == 0 and qk_cols % tn == 0 and q_cols % tn == 0
    conv_blocks = c // tn
    return pl.pallas_call(
        functools.partial(_gdn_proj_kernel, tn=tn, q_blocks=q_cols // tn, qk_blocks=qk_cols // tn,
                          conv_blocks=conv_blocks),
        grid=(s // tm, n_cols // tn),
        in_specs=[
            pl.BlockSpec((tm, d), lambda m, j: (m, 0)),
            pl.BlockSpec((1, d), lambda m, j: (0, 0)),
            pl.BlockSpec((None, d, tn), lambda m, j: (layer, 0, j)),
            pl.BlockSpec((taps, tn), lambda m, j: (0, jnp.minimum(j, conv_blocks - 1))),
        ],
        out_specs=pl.BlockSpec((tm, tn), lambda m, j: (m, j)),
        out_shape=jax.ShapeDtypeStruct((s, n_cols), BF16),
        scratch_shapes=[pltpu.VMEM((tm, d), BF16), pltpu.VMEM((conv_blocks, 8, tn), F32)],
        compiler_params=_cparams("arbitrary", "arbitrary"),
        name="gdn_norm_proj_conv",
    )(h, gain.reshape(1, d), w, conv_w)


def _gdn_gate_kernel(x_ref, alog_ref, dtb_ref, o_ref, *, v_heads):
    x = x_ref[...]
    beta = jax.nn.sigmoid(x)
    g = -jnp.exp(alog_ref[...]) * jax.nn.softplus(x + dtb_ref[...])
    lane = lax.broadcasted_iota(I32, x.shape, 1)
    o_ref[...] = jnp.where(lane < v_heads, beta, g).T


def _gdn_gates(ba, a_log, dt_bias, v_heads, tb=512):
    s = ba.shape[0]
    pad = LANES - 2 * v_heads
    alog = jnp.concatenate([jnp.zeros((v_heads,), F32), a_log, jnp.zeros((pad,), F32)]).reshape(1, LANES)
    dtb = jnp.concatenate([jnp.zeros((v_heads,), F32), dt_bias, jnp.zeros((pad,), F32)]).reshape(1, LANES)
    return pl.pallas_call(
        functools.partial(_gdn_gate_kernel, v_heads=v_heads),
        grid=(s // tb,),
        in_specs=[pl.BlockSpec((tb, LANES), lambda i: (i, 0)),
                  pl.BlockSpec((1, LANES), lambda i: (0, 0)),
                  pl.BlockSpec((1, LANES), lambda i: (0, 0))],
        out_specs=pl.BlockSpec((LANES, tb), lambda i: (0, i)),
        out_shape=jax.ShapeDtypeStruct((LANES, s), F32),
        compiler_params=_cparams("parallel"),
        name="gdn_gates",
    )(ba, alog, dtb)


def _unit_lower_inverses(lows, ii, jj):
    c = lows[0].shape[0]
    xor = ii ^ jj
    eye = (ii == jj).astype(F32)
    xs = [eye - jnp.where(xor < 2, low, 0.0) for low in lows]
    b = 2
    while b < c:
        lower_left = (xor >= b) & (xor < 2 * b)
        offs = [jnp.where(lower_left, low, 0.0).astype(BF16) for low in lows]
        xbs = [x.astype(BF16) for x in xs]
        ys = [_dot(xb, off).astype(BF16) for xb, off in zip(xbs, offs)]
        xs = [x - _dot(y, xb) for x, y, xb in zip(xs, ys, xbs)]
        b *= 2
    return xs


def _gdn_delta_kernel(q_ref, k_ref, v_ref, z_ref, beta_ref, g_ref, gain_ref, o_ref,
                      state_ref, u_ref, wq_ref, intra_ref, kdt_ref, *, n_chunks, k_heads, rep, par):
    c_len = DELTA_CHUNK

    @pl.when(pl.program_id(1) == 0)
    def _():
        state_ref[...] = jnp.zeros(state_ref.shape, F32)

    def chunk_local(kq, cg):
        ii = lax.broadcasted_iota(I32, (c_len, c_len), 0)
        jj = lax.broadcasted_iota(I32, (c_len, c_len), 1)
        causal = jj <= ii
        strict = jj < ii
        diag = jj == ii
        qk_cols = slice(kq * HEAD, (kq + 1) * HEAD)
        cs = [cg * par + cc for cc in range(par)]
        rows = [pl.ds(pl.multiple_of(c * c_len, c_len), c_len) for c in cs]
        ks = [k_ref[r, qk_cols] for r in rows]
        qs = [q_ref[r, qk_cols] for r in rows]
        kks = [_dot_nt(k, k) for k in ks]
        qks = [_dot_nt(q, k) for q, k in zip(qs, ks)]
        kfs = [k.astype(F32) for k in ks]
        kts = [kf.T for kf in kfs]
        chains = [(cc, kq * rep + e) for cc in range(par) for e in range(rep)]
        g_rows = [g_ref[hv, cs[cc]] for cc, hv in chains]
        b_rows = [beta_ref[hv, cs[cc]] for cc, hv in chains]
        gc_cols = [jnp.sum(jnp.where(causal, g, 0.0), axis=1, keepdims=True) for g in g_rows]
        gc_rows = [jnp.sum(jnp.where(diag, gc, 0.0), axis=0, keepdims=True) for gc in gc_cols]
        b_cols = [jnp.sum(jnp.where(diag, b, 0.0), axis=1, keepdims=True) for b in b_rows]
        g_lasts = [jnp.sum(g, axis=1, keepdims=True) for g in g_rows]
        decays = [jnp.where(causal, jnp.exp(gc - gr), 0.0) for gc, gr in zip(gc_cols, gc_rows)]
        lows = [jnp.where(strict, bc * kks[cc] * dec, 0.0)
                for (cc, hv), bc, dec in zip(chains, b_cols, decays)]
        t_invs = _unit_lower_inverses(lows, ii, jj)
        for n, (cc, hv) in enumerate(chains):
            c = cs[cc]
            e_gc = jnp.exp(gc_cols[n])
            v = v_ref[rows[cc], hv * HEAD:(hv + 1) * HEAD].astype(F32)
            kb = kfs[cc] * b_cols[n]
            rhs = jnp.concatenate([v * b_cols[n], kb * e_gc], axis=1).astype(BF16)
            sol = _dot(t_invs[n].astype(BF16), rhs)
            u_ref[c, hv] = sol[:, :HEAD]
            wq_ref[c, hv, 0:c_len, :] = sol[:, HEAD:].astype(BF16)
            wq_ref[c, hv, c_len:2 * c_len, :] = (qs[cc].astype(F32) * e_gc).astype(BF16)
            intra_ref[c, hv] = (qks[cc] * decays[n]).astype(BF16)
            kdt_ref[c, hv] = (kts[cc] * jnp.exp(g_lasts[n] - gc_rows[n])).astype(BF16)

    for kq in range(k_heads):
        def local_body(cg, carry, kq=kq):
            chunk_local(kq, cg)
            return carry
        lax.fori_loop(0, n_chunks // par, local_body, 0)

    gain = gain_ref[...]
    heads = range(k_heads * rep)

    def chunk_recurrent(c, carry):
        rows = pl.ds(pl.multiple_of(c * c_len, c_len), c_len)
        states = [state_ref[hv] for hv in heads]
        ws_qs = [_dot(wq_ref[c, hv], states[hv].astype(BF16)) for hv in heads]
        v_news = [(u_ref[c, hv] - ws_qs[hv][:c_len]).astype(BF16) for hv in heads]
        upds = [_dot(kdt_ref[c, hv], v_news[hv]) for hv in heads]
        for hv in heads:
            decay_all = jnp.exp(jnp.sum(g_ref[hv, c], axis=1, keepdims=True))
            state_ref[hv] = states[hv] * decay_all + upds[hv]
        for hv in heads:
            o = ws_qs[hv][c_len:] + _dot(intra_ref[c, hv], v_news[hv])
            z = z_ref[rows, hv * HEAD:(hv + 1) * HEAD].astype(F32)
            o = _rms_rows(o) * gain * (z * jax.nn.sigmoid(z))
            o_ref[rows, hv * HEAD:(hv + 1) * HEAD] = o.astype(o_ref.dtype)
        return carry

    lax.fori_loop(0, n_chunks, chunk_recurrent, 0)


def _gdn_delta(qkv, proj, gates, out_gain, qk_heads, v_heads, z_col0, tt=1024, par=8, k_heads=2):
    s = qkv.shape[0]
    rep = v_heads // qk_heads
    c_len = DELTA_CHUNK
    n_chunks = tt // c_len
    g4 = gates.reshape(LANES, s // c_len, 1, c_len)
    qw = k_heads * HEAD
    vw = k_heads * rep * HEAD
    nh = k_heads * rep
    return pl.pallas_call(
        functools.partial(_gdn_delta_kernel, n_chunks=n_chunks, k_heads=k_heads, rep=rep, par=par),
        grid=(qk_heads // k_heads, s // tt),
        in_specs=[
            pl.BlockSpec((tt, qw), lambda h, t: (t, h)),
            pl.BlockSpec((tt, qw), lambda h, t: (t, qk_heads // k_heads + h)),
            pl.BlockSpec((tt, vw), lambda h, t: (t, (2 * qk_heads * HEAD) // vw + h)),
            pl.BlockSpec((tt, vw), lambda h, t: (t, z_col0 // vw + h)),
            pl.BlockSpec((nh, n_chunks, 1, c_len), lambda h, t: (h, t, 0, 0)),
            pl.BlockSpec((nh, n_chunks, 1, c_len), lambda h, t: (v_heads // nh + h, t, 0, 0)),
            pl.BlockSpec((1, HEAD), lambda h, t: (0, 0)),
        ],
        out_specs=pl.BlockSpec((tt, vw), lambda h, t: (t, h)),
        out_shape=jax.ShapeDtypeStruct((s, v_heads * HEAD), BF16),
        scratch_shapes=[pltpu.VMEM((nh, HEAD, HEAD), F32),
                        pltpu.VMEM((n_chunks, nh, c_len, HEAD), F32),
                        pltpu.VMEM((n_chunks, nh, 2 * c_len, HEAD), BF16),
                        pltpu.VMEM((n_chunks, nh, c_len, c_len), BF16),
                        pltpu.VMEM((n_chunks, nh, HEAD, c_len), BF16)],
        compiler_params=_cparams("parallel", "arbitrary"),
        name="gdn_delta",
    )(qkv, qkv, qkv, proj, g4, g4, out_gain.reshape(1, HEAD))


def _pad_cols(w, width):
    return jnp.pad(w, ((0, 0), (0, width - w.shape[1])))


def _mixer_a(h, gain, w_in, w_out, slot, lam_q1, lam_k1, lam_q2, lam_k2, sub_norm, lambda_init):
    d = h.shape[1]
    heads = d // (2 * HEAD)
    qk_cols = 2 * heads * HEAD
    colscale = jnp.concatenate([jnp.full((qk_cols,), LOG2_E * HEAD ** -0.5, F32),
                                jnp.ones((w_in.shape[2] - qk_cols,), F32)])
    qkv = _norm_proj(h, gain, w_in, slot, colscale=colscale)
    lam_params = jnp.stack([lam_q1, lam_k1, lam_q2, lam_k2]).astype(F32)
    o = _diff_attention(qkv, lam_params, sub_norm, lambda_init, heads)
    return _out_proj(o, w_out, slot, h)


def _mixer_b(h, gain, w_in, w_in_tail, w_out, slot, forget_bias, q_norm, k_norm):
    d = h.shape[1]
    heads = d // HEAD
    width = heads * HEAD
    colscale = jnp.concatenate([jnp.tile(q_norm, heads) * (LOG2_E * HEAD ** -0.5), jnp.tile(k_norm, heads),
                                jnp.ones((2 * width,), F32)])
    qkvg = _norm_proj(h, gain, w_in, slot, n_cols=4 * width, colscale=colscale, norm_cols=2 * width)
    f_logit = _norm_proj(h, gain, _pad_cols(w_in_tail, LANES).astype(BF16)[None], 0, out_dtype=F32)
    bias = jnp.pad(forget_bias, (0, LANES - heads)).reshape(1, LANES)
    cum_t = _fox_cum_log_forget(f_logit, bias)[:heads].reshape(heads, 1, -1)
    o = _fox_attention(qkvg, cum_t, heads)
    return _out_proj(o, w_out, slot, h)


def _mixer_c(h, gain, w_in, w_in_tail, w_out, slot, conv_w, a_log, dt_bias, out_norm):
    d = h.shape[1]
    qk_heads = d // HEAD
    v_heads = 2 * qk_heads
    qk_w = qk_heads * HEAD
    v_w = v_heads * HEAD
    main = 2 * qk_w + 2 * v_w
    proj = _gdn_norm_proj_conv(h, gain, w_in, slot, conv_w, n_cols=main, qk_cols=2 * qk_w,
                               q_cols=qk_w)
    ba = _norm_proj(h, gain, _pad_cols(w_in_tail, LANES).astype(BF16)[None], 0, out_dtype=F32)
    gates = _gdn_gates(ba, a_log, dt_bias, v_heads)
    o = _gdn_delta(proj, proj, gates, out_norm, qk_heads, v_heads, z_col0=2 * qk_w + v_w)
    return _out_proj(o, w_out, slot, h)


def kernel(x, mix_norm, ffn_norm, final_norm, a_w_in, a_w_out, a_lam_q1, a_lam_k1, a_lam_q2, a_lam_k2,
           a_sub_norm, b_w_in, b_w_out, b_forget_bias, b_q_norm, b_k_norm, c_w_in, c_w_out, c_conv_w,
           c_a_log, c_dt_bias, c_out_norm, ffn_w_gate, ffn_w_up, ffn_w_down):
    batch, seq, d = x.shape
    depth = mix_norm.shape[0]
    a_in, a_out = a_w_in.astype(BF16), a_w_out.astype(BF16)
    b_in, b_out = b_w_in.astype(BF16), b_w_out.astype(BF16)
    c_in, c_out = c_w_in.astype(BF16), c_w_out.astype(BF16)
    w_gate, w_up, w_down = ffn_w_gate.astype(BF16), ffn_w_up.astype(BF16), ffn_w_down.astype(BF16)
    b_main = 4 * d
    c_main = 6 * d
    outs = []
    for b in range(batch):
        h = x.reshape(seq, d) if batch == 1 else x[b]
        for i in range(depth):
            slot = i // N_MIXERS
            if i % N_MIXERS == 0:
                lambda_init = 0.8 - 0.6 * math.exp(-0.3 * i)
                h = _mixer_a(h, mix_norm[i], a_in, a_out, slot, a_lam_q1[slot], a_lam_k1[slot],
                             a_lam_q2[slot], a_lam_k2[slot], a_sub_norm[slot], lambda_init)
            elif i % N_MIXERS == 1:
                h = _mixer_b(h, mix_norm[i], b_in, b_w_in[slot][:, b_main:], b_out, slot, b_forget_bias[slot],
                             b_q_norm[slot], b_k_norm[slot])
            else:
                h = _mixer_c(h, mix_norm[i], c_in, c_w_in[slot][:, c_main:], c_out, slot, c_conv_w[slot],
                             c_a_log[slot], c_dt_bias[slot], c_out_norm[slot])
            h = _ffn(h, ffn_norm[i], w_gate, w_up, w_down, i,
                     final_gain=final_norm if i == depth - 1 else None)
        outs.append(h)
    return outs[0].reshape(1, seq, d) if batch == 1 else jnp.stack(outs)
```

```python
import functools
import math

import jax
import jax.numpy as jnp
from jax import lax
from jax.experimental import pallas as pl
from jax.experimental.pallas import tpu as pltpu

F32 = jnp.float32
BF16 = jnp.bfloat16
I32 = jnp.int32

EPS = 1e-6
MASK_CHUNK = 64
HEAD = 128
LANES = 128
N_MIXERS = 3
LOG2_E = math.log2(math.e)
DELTA_CHUNK = 128
VMEM_LIMIT_BYTES = 52 * 1024 * 1024


def _cparams(*sem):
    return pltpu.CompilerParams(dimension_semantics=sem, vmem_limit_bytes=VMEM_LIMIT_BYTES)


def _dot(a, b):
    return jnp.dot(a, b, preferred_element_type=F32)


def _dot_nt(a, b):
    return lax.dot_general(a, b, (((1,), (1,)), ((), ())), preferred_element_type=F32)


def _rms_rows(x):
    return x * lax.rsqrt(jnp.mean(x * x, axis=-1, keepdims=True) + EPS)


def _proj_kernel(*refs, tn, norm_blocks, has_colscale):
    if has_colscale:
        x_ref, g_ref, w_ref, cs_ref, o_ref, xn_ref = refs
    else:
        x_ref, g_ref, w_ref, o_ref, xn_ref = refs
        cs_ref = None
    n = pl.program_id(1)

    @pl.when(n == 0)
    def _():
        xn_ref[...] = (_rms_rows(x_ref[...]) * g_ref[...]).astype(BF16)

    acc = _dot(xn_ref[...], w_ref[...])

    def finish(a):
        if cs_ref is not None:
            a = a * cs_ref[...]
        o_ref[...] = a.astype(o_ref.dtype)

    if norm_blocks == 0:
        finish(acc)
    else:
        @pl.when(n < norm_blocks)
        def _():
            parts = [_rms_rows(acc[:, c * HEAD:(c + 1) * HEAD]) for c in range(tn // HEAD)]
            finish(jnp.concatenate(parts, axis=1))

        @pl.when(n >= norm_blocks)
        def _():
            finish(acc)


def _norm_proj(h, gain, w, layer, n_cols=None, colscale=None, norm_cols=0, out_dtype=BF16, tm=1024, tn=1024):
    s, d = h.shape
    n = w.shape[2] if n_cols is None else n_cols
    tn = min(tn, n)
    assert s % tm == 0 and n % tn == 0 and norm_cols % tn == 0 and tn % HEAD == 0
    in_specs = [
        pl.BlockSpec((tm, d), lambda m, j: (m, 0)),
        pl.BlockSpec((1, d), lambda m, j: (0, 0)),
        pl.BlockSpec((None, d, tn), lambda m, j: (layer, 0, j)),
    ]
    args = [h, gain.reshape(1, d), w]
    if colscale is not None:
        in_specs.append(pl.BlockSpec((1, tn), lambda m, j: (0, j)))
        args.append(colscale.reshape(1, n))
    return pl.pallas_call(
        functools.partial(_proj_kernel, tn=tn, norm_blocks=norm_cols // tn,
                          has_colscale=colscale is not None),
        grid=(s // tm, n // tn),
        in_specs=in_specs,
        out_specs=pl.BlockSpec((tm, tn), lambda m, j: (m, j)),
        out_shape=jax.ShapeDtypeStruct((s, n), out_dtype),
        scratch_shapes=[pltpu.VMEM((tm, d), BF16)],
        compiler_params=_cparams("parallel", "arbitrary"),
        name="norm_proj",
    )(*args)


def _out_proj_kernel(o_ref, w_ref, h_ref, y_ref):
    y_ref[...] = h_ref[...] + _dot(o_ref[...], w_ref[...])


def _out_proj(o, w, layer, h, tm=1024, tn=None):
    s, k = o.shape
    d = w.shape[2]
    if tn is None:
        tn = 1024 if k <= 2048 else 512
    return pl.pallas_call(
        _out_proj_kernel,
        grid=(s // tm, d // tn),
        in_specs=[
            pl.BlockSpec((tm, k), lambda m, j: (m, 0)),
            pl.BlockSpec((None, k, tn), lambda m, j: (layer, 0, j)),
            pl.BlockSpec((tm, tn), lambda m, j: (m, j)),
        ],
        out_specs=pl.BlockSpec((tm, tn), lambda m, j: (m, j)),
        out_shape=jax.ShapeDtypeStruct((s, d), F32),
        compiler_params=_cparams("parallel", "arbitrary"),
        name="out_proj",
    )(o, w, h)


def _ffn_kernel(*refs, has_final):
    if has_final:
        h_ref, g_ref, wg_ref, wu_ref, wd_ref, fg_ref, y_ref, xn_ref = refs
    else:
        h_ref, g_ref, wg_ref, wu_ref, wd_ref, y_ref, xn_ref = refs
        fg_ref = None
    f = pl.program_id(1)

    @pl.when(f == 0)
    def _():
        h = h_ref[...]
        xn_ref[...] = (_rms_rows(h) * g_ref[...]).astype(BF16)
        y_ref[...] = h

    xn = xn_ref[...]
    gate = _dot(xn, wg_ref[...])
    up = _dot(xn, wu_ref[...])
    act = (gate * jax.nn.sigmoid(gate) * up).astype(BF16)
    y_ref[...] += _dot(act, wd_ref[...])

    if fg_ref is not None:
        @pl.when(f == pl.num_programs(1) - 1)
        def _():
            y_ref[...] = _rms_rows(y_ref[...]) * fg_ref[...]


def _ffn(h, gain, w_gate, w_up, w_down, layer, final_gain=None, tm=512, tf=512):
    s, d = h.shape
    dff = w_gate.shape[2]
    in_specs = [
        pl.BlockSpec((tm, d), lambda m, f: (m, 0)),
        pl.BlockSpec((1, d), lambda m, f: (0, 0)),
        pl.BlockSpec((None, d, tf), lambda m, f: (layer, 0, f)),
        pl.BlockSpec((None, d, tf), lambda m, f: (layer, 0, f)),
        pl.BlockSpec((None, tf, d), lambda m, f: (layer, f, 0)),
    ]
    args = [h, gain.reshape(1, d), w_gate, w_up, w_down]
    if final_gain is not None:
        in_specs.append(pl.BlockSpec((1, d), lambda m, f: (0, 0)))
        args.append(final_gain.reshape(1, d))
    return pl.pallas_call(
        functools.partial(_ffn_kernel, has_final=final_gain is not None),
        grid=(s // tm, dff // tf),
        in_specs=in_specs,
        out_specs=pl.BlockSpec((tm, d), lambda m, f: (m, 0)),
        out_shape=jax.ShapeDtypeStruct((s, d), F32),
        scratch_shapes=[pltpu.VMEM((tm, d), BF16)],
        compiler_params=_cparams("parallel", "arbitrary"),
        name="swiglu",
    )(*args)


def _kv_block_start(t, q0, tk):
    return pl.multiple_of(jnp.where(t < 2, q0 + t * tk, (t - 2) * tk), tk)


def _run_kv_pipeline(n_full_pairs, scores, softmax, weighted_values):
    scores(0, 0, True)
    scores(1, 1, True)
    softmax(0)

    def body(u, carry):
        t = 2 * u
        scores(t + 2, 0, False)
        softmax(1)
        weighted_values(t, 0)
        scores(t + 3, 1, False)
        softmax(0)
        weighted_values(t + 1, 1)
        return carry

    lax.fori_loop(0, n_full_pairs, body, 0)
    n = 2 + 2 * n_full_pairs
    softmax(1)
    weighted_values(n - 2, 0)
    weighted_values(n - 1, 1)


def _diff_attn_kernel(q1_ref, q2_ref, k1_ref, k2_ref, v_ref, lam_ref, sn_ref, o_ref,
                      s_ref, p_ref, a_ref, m_ref, l_ref, acc_ref, *, tq, tk, lambda_init):
    i = pl.program_id(1)
    q0 = pl.multiple_of(i * tq, tq)
    lane_tiles = tk // LANES
    v_tiles = v_ref.shape[1] // LANES
    m_ref[...] = jnp.full(m_ref.shape, -jnp.inf, F32)
    l_ref[...] = jnp.zeros(l_ref.shape, F32)
    acc_ref[...] = jnp.zeros(acc_ref.shape, F32)
    qs = (q1_ref[...], q2_ref[...])
    ks = (k1_ref, k2_ref)

    def scores(t, slot, masked):
        k0 = _kv_block_start(t, q0, tk)
        if masked:
            row = lax.broadcasted_iota(I32, (tq, tk), 0)
            col = lax.broadcasted_iota(I32, (tq, tk), 1) + (k0 - q0)
            visible = (col // MASK_CHUNK) <= (row // MASK_CHUNK)
        for mp in range(2):
            s = _dot_nt(qs[mp], ks[mp][pl.ds(k0, tk), :])
            if masked:
                s = jnp.where(visible, s, -jnp.inf)
            s_ref[slot, mp] = s

    def softmax(slot):
        for mp in range(2):
            s = s_ref[slot, mp]
            m_prev = m_ref[mp]
            m_new = jnp.maximum(m_prev, jnp.max(s, axis=-1, keepdims=True))
            alpha = jnp.exp2(m_prev - m_new)
            l_new = alpha * l_ref[mp]
            for c in range(lane_tiles):
                p = jnp.exp2(s[:, c * LANES:(c + 1) * LANES] - m_new)
                l_new = l_new + p
                p_ref[slot, mp, :, c * LANES:(c + 1) * LANES] = p.astype(BF16)
            l_ref[mp] = l_new
            m_ref[mp] = m_new
            a_ref[slot, mp] = alpha

    def weighted_values(t, slot):
        k0 = _kv_block_start(t, q0, tk)
        v = v_ref[pl.ds(k0, tk), :]
        for mp in range(2):
            alpha = jnp.concatenate([a_ref[slot, mp]] * v_tiles, axis=1)
            acc_ref[mp] = alpha * acc_ref[mp] + _dot(p_ref[slot, mp], v)

    _run_kv_pipeline(i * (tq // (2 * tk)), scores, softmax, weighted_values)

    lam = (jnp.exp(jnp.sum(lam_ref[0:1, :] * lam_ref[1:2, :], keepdims=True))
           - jnp.exp(jnp.sum(lam_ref[2:3, :] * lam_ref[3:4, :], keepdims=True)) + lambda_init)
    l1 = jnp.sum(l_ref[0], axis=-1, keepdims=True)
    l2 = jnp.sum(l_ref[1], axis=-1, keepdims=True)
    o = acc_ref[0] / l1 - lam * (acc_ref[1] / l2)
    o = _rms_rows(o) * sn_ref[...] * (1.0 - lambda_init)
    o_ref[...] = o.astype(o_ref.dtype)


def _diff_attention(qkv, lam_params, sub_norm, lambda_init, heads, tq=512):
    s = qkv.shape[0]
    vd = 2 * HEAD
    tk = tq // 2
    hb = heads
    return pl.pallas_call(
        functools.partial(_diff_attn_kernel, tq=tq, tk=tk, lambda_init=lambda_init),
        grid=(heads, s // tq),
        in_specs=[
            pl.BlockSpec((tq, HEAD), lambda h, i: (i, h)),
            pl.BlockSpec((tq, HEAD), lambda h, i: (i, hb + h)),
            pl.BlockSpec((s, HEAD), lambda h, i: (0, 2 * hb + h)),
            pl.BlockSpec((s, HEAD), lambda h, i: (0, 3 * hb + h)),
            pl.BlockSpec((s, vd), lambda h, i: (0, 2 * hb + h)),
            pl.BlockSpec((4, HEAD), lambda h, i: (0, 0)),
            pl.BlockSpec((1, vd), lambda h, i: (0, 0)),
        ],
        out_specs=pl.BlockSpec((tq, vd), lambda h, i: (i, h)),
        out_shape=jax.ShapeDtypeStruct((s, heads * vd), BF16),
        scratch_shapes=[pltpu.VMEM((2, 2, tq, tk), F32), pltpu.VMEM((2, 2, tq, tk), BF16),
                        pltpu.VMEM((2, 2, tq, LANES), F32), pltpu.VMEM((2, tq, LANES), F32),
                        pltpu.VMEM((2, tq, LANES), F32), pltpu.VMEM((2, tq, vd), F32)],
        compiler_params=_cparams("parallel", "arbitrary"),
        name="diff_attention",
    )(qkv, qkv, qkv, qkv, qkv, lam_params, sub_norm.reshape(1, vd))


def _fox_gate_kernel(fl_ref, b_ref, o_ref, carry_ref, *, tb):
    @pl.when(pl.program_id(0) == 0)
    def _():
        carry_ref[...] = jnp.zeros(carry_ref.shape, F32)

    log_f = jax.nn.log_sigmoid(fl_ref[...] + b_ref[...])
    tri = (lax.broadcasted_iota(I32, (tb, tb), 1) <= lax.broadcasted_iota(I32, (tb, tb), 0)).astype(F32)
    cum = jnp.dot(tri, log_f, preferred_element_type=F32,
                  precision=lax.Precision.HIGHEST) + carry_ref[...]
    carry_ref[...] = cum[tb - 1:tb, :]
    o_ref[...] = cum.T


def _fox_cum_log_forget(f_logit, bias, tb=256):
    s = f_logit.shape[0]
    return pl.pallas_call(
        functools.partial(_fox_gate_kernel, tb=tb),
        grid=(s // tb,),
        in_specs=[pl.BlockSpec((tb, LANES), lambda i: (i, 0)),
                  pl.BlockSpec((1, LANES), lambda i: (0, 0))],
        out_specs=pl.BlockSpec((LANES, tb), lambda i: (0, i)),
        out_shape=jax.ShapeDtypeStruct((LANES, s), F32),
        scratch_shapes=[pltpu.VMEM((1, LANES), F32)],
        compiler_params=_cparams("arbitrary"),
        name="fox_cum_log_forget",
    )(f_logit, bias)


def _fox_attn_kernel(q_ref, k_ref, v_ref, gate_ref, ck_ref, o_ref,
                     s_ref, p_ref, a_ref, m_ref, l_ref, acc_ref, *, tq, tk):
    i = pl.program_id(1)
    q0 = pl.multiple_of(i * tq, tq)
    lane_tiles = tk // LANES
    m_ref[...] = jnp.full(m_ref.shape, -jnp.inf, F32)
    l_ref[...] = jnp.zeros(l_ref.shape, F32)
    acc_ref[...] = jnp.zeros(acc_ref.shape, F32)
    q = q_ref[...]
    base = ck_ref[:, pl.ds(q0, LANES)][:, 0:1]

    def scores(t, slot, masked):
        k0 = _kv_block_start(t, q0, tk)
        s = _dot_nt(q, k_ref[pl.ds(k0, tk), :]) + (base - ck_ref[:, pl.ds(k0, tk)]) * LOG2_E
        if masked:
            row = lax.broadcasted_iota(I32, (tq, tk), 0)
            col = lax.broadcasted_iota(I32, (tq, tk), 1)
            s = jnp.where(col + (k0 - q0) <= row, s, -jnp.inf)
        s_ref[slot] = s

    def softmax(slot):
        s = s_ref[slot]
        m_prev = m_ref[...]
        m_new = jnp.maximum(m_prev, jnp.max(s, axis=-1, keepdims=True))
        alpha = jnp.exp2(m_prev - m_new)
        l_new = alpha * l_ref[...]
        for c in range(lane_tiles):
            p = jnp.exp2(s[:, c * LANES:(c + 1) * LANES] - m_new)
            l_new = l_new + p
            p_ref[slot, :, c * LANES:(c + 1) * LANES] = p.astype(BF16)
        l_ref[...] = l_new
        m_ref[...] = m_new
        a_ref[slot] = alpha

    def weighted_values(t, slot):
        k0 = _kv_block_start(t, q0, tk)
        acc_ref[...] = a_ref[slot] * acc_ref[...] + _dot(p_ref[slot], v_ref[pl.ds(k0, tk), :])

    _run_kv_pipeline(i * (tq // (2 * tk)), scores, softmax, weighted_values)

    l = jnp.sum(l_ref[...], axis=-1, keepdims=True)
    o = acc_ref[...] / l * jax.nn.sigmoid(gate_ref[...].astype(F32))
    o_ref[...] = o.astype(o_ref.dtype)


def _fox_attention(qkvg, cum_t, heads, tq=1024):
    s = qkvg.shape[0]
    tk = tq // 2
    return pl.pallas_call(
        functools.partial(_fox_attn_kernel, tq=tq, tk=tk),
        grid=(heads, s // tq),
        in_specs=[
            pl.BlockSpec((tq, HEAD), lambda h, i: (i, h)),
            pl.BlockSpec((s, HEAD), lambda h, i: (0, heads + h)),
            pl.BlockSpec((s, HEAD), lambda h, i: (0, 2 * heads + h)),
            pl.BlockSpec((tq, HEAD), lambda h, i: (i, 3 * heads + h)),
            pl.BlockSpec((None, 1, s), lambda h, i: (h, 0, 0)),
        ],
        out_specs=pl.BlockSpec((tq, HEAD), lambda h, i: (i, h)),
        out_shape=jax.ShapeDtypeStruct((s, heads * HEAD), BF16),
        scratch_shapes=[pltpu.VMEM((2, tq, tk), F32), pltpu.VMEM((2, tq, tk), BF16),
                        pltpu.VMEM((2, tq, LANES), F32), pltpu.VMEM((tq, LANES), F32),
                        pltpu.VMEM((tq, LANES), F32), pltpu.VMEM((tq, HEAD), F32)],
        compiler_params=_cparams("parallel", "arbitrary"),
        name="fox_attention",
    )(qkvg, qkvg, qkvg, qkvg, cum_t)


def _gdn_proj_kernel(x_ref, g_ref, w_ref, cw_ref, o_ref, xn_ref, halo_ref, *,
                     tn, q_blocks, qk_blocks, conv_blocks):
    m = pl.program_id(0)
    n = pl.program_id(1)

    @pl.when(n == 0)
    def _():
        xn_ref[...] = (_rms_rows(x_ref[...]) * g_ref[...]).astype(BF16)

    acc = _dot(xn_ref[...], w_ref[...])
    tm = acc.shape[0]

    def conv_silu():
        w = cw_ref[...]
        taps = w.shape[0]
        halo = jnp.where(m > 0, halo_ref[n], 0.0)
        halo_ref[n] = acc[tm - 8:tm, :]
        x_top = acc[0:8, :]
        row8 = lax.broadcasted_iota(I32, (8, tn), 0)
        y = acc * w[taps - 1:taps, :]
        y_top = x_top * w[taps - 1:taps, :]
        for sh in range(1, taps):
            wj = w[taps - 1 - sh:taps - sh, :]
            y = y + pltpu.roll(acc, sh, 0) * wj
            before = pltpu.roll(halo, sh, 0)
            y_top = y_top + jnp.where(row8 < sh, before, pltpu.roll(x_top, sh, 0)) * wj
        return y * jax.nn.sigmoid(y), y_top * jax.nn.sigmoid(y_top)

    def emit(transform):
        y, y_top = conv_silu()
        o_ref[...] = transform(y).astype(o_ref.dtype)
        o_ref[0:8, :] = transform(y_top).astype(o_ref.dtype)

    def l2_heads(scale):
        def transform(a):
            parts = []
            for c in range(tn // HEAD):
                b = a[:, c * HEAD:(c + 1) * HEAD]
                parts.append(b * (lax.rsqrt(jnp.sum(b * b, axis=-1, keepdims=True) + EPS) * scale))
            return jnp.concatenate(parts, axis=1)
        return transform

    @pl.when(n < q_blocks)
    def _():
        emit(l2_heads(HEAD ** -0.5))

    @pl.when((n >= q_blocks) & (n < qk_blocks))
    def _():
        emit(l2_heads(1.0))

    @pl.when((n >= qk_blocks) & (n < conv_blocks))
    def _():
        emit(lambda a: a)

    @pl.when(n >= conv_blocks)
    def _():
        o_ref[...] = acc.astype(o_ref.dtype)


def _gdn_norm_proj_conv(h, gain, w, layer, conv_w, n_cols, qk_cols, q_cols, tm=1024, tn=1024):
    s, d = h.shape
    taps, c = conv_w.shape
    assert s % tm == 0 and n_cols % tn == 0 and c % tn == 0 and qk_cols % tn == 0 and q_cols % tn == 0
    conv_blocks = c // tn
    return pl.pallas_call(
        functools.partial(_gdn_proj_kernel, tn=tn, q_blocks=q_cols // tn, qk_blocks=qk_cols // tn,
                          conv_blocks=conv_blocks),
        grid=(s // tm, n_cols // tn),
        in_specs=[
            pl.BlockSpec((tm, d), lambda m, j: (m, 0)),
            pl.BlockSpec((1, d), lambda m, j: (0, 0)),
            pl.BlockSpec((None, d, tn), lambda m, j: (layer, 0, j)),
            pl.BlockSpec((taps, tn), lambda m, j: (0, jnp.minimum(j, conv_blocks - 1))),
        ],
        out_specs=pl.BlockSpec((tm, tn), lambda m, j: (m, j)),
        out_shape=jax.ShapeDtypeStruct((s, n_cols), BF16),
        scratch_shapes=[pltpu.VMEM((tm, d), BF16), pltpu.VMEM((conv_blocks, 8, tn), F32)],
        compiler_params=_cparams("arbitrary", "arbitrary"),
        name="gdn_norm_proj_conv",
    )(h, gain.reshape(1, d), w, conv_w)


def _gdn_gate_kernel(x_ref, alog_ref, dtb_ref, o_ref, *, v_heads):
    x = x_ref[...]
    beta = jax.nn.sigmoid(x)
    g = -jnp.exp(alog_ref[...]) * jax.nn.softplus(x + dtb_ref[...])
    lane = lax.broadcasted_iota(I32, x.shape, 1)
    o_ref[...] = jnp.where(lane < v_heads, beta, g).T


def _gdn_gates(ba, a_log, dt_bias, v_heads, tb=512):
    s = ba.shape[0]
    pad = LANES - 2 * v_heads
    alog = jnp.concatenate([jnp.zeros((v_heads,), F32), a_log, jnp.zeros((pad,), F32)]).reshape(1, LANES)
    dtb = jnp.concatenate([jnp.zeros((v_heads,), F32), dt_bias, jnp.zeros((pad,), F32)]).reshape(1, LANES)
    return pl.pallas_call(
        functools.partial(_gdn_gate_kernel, v_heads=v_heads),
        grid=(s // tb,),
        in_specs=[pl.BlockSpec((tb, LANES), lambda i: (i, 0)),
                  pl.BlockSpec((1, LANES), lambda i: (0, 0)),
                  pl.BlockSpec((1, LANES), lambda i: (0, 0))],
        out_specs=pl.BlockSpec((LANES, tb), lambda i: (0, i)),
        out_shape=jax.ShapeDtypeStruct((LANES, s), F32),
        compiler_params=_cparams("parallel"),
        name="gdn_gates",
    )(ba, alog, dtb)


def _unit_lower_inverses(lows, ii, jj):
    c = lows[0].shape[0]
    xor = ii ^ jj
    eye = (ii == jj).astype(F32)
    xs = [eye - jnp.where(xor < 2, low, 0.0) for low in lows]
    b = 2
    while b < c:
        lower_left = (xor >= b) & (xor < 2 * b)
        offs = [jnp.where(lower_left, low, 0.0).astype(BF16) for low in lows]
        xbs = [x.astype(BF16) for x in xs]
        ys = [_dot(xb, off).astype(BF16) for xb, off in zip(xbs, offs)]
        xs = [x - _dot(y, xb) for x, y, xb in zip(xs, ys, xbs)]
        b *= 2
    return xs


def _gdn_delta_kernel(q_ref, k_ref, v_ref, z_ref, beta_ref, g_ref, gain_ref, o_ref,
                      state_ref, u_ref, wq_ref, intra_ref, kdt_ref, *, n_chunks, k_heads, rep, par):
    c_len = DELTA_CHUNK

    @pl.when(pl.program_id(1) == 0)
    def _():
        state_ref[...] = jnp.zeros(state_ref.shape, F32)

    def chunk_local(kq, cg):
        ii = lax.broadcasted_iota(I32, (c_len, c_len), 0)
        jj = lax.broadcasted_iota(I32, (c_len, c_len), 1)
        causal = jj <= ii
        strict = jj < ii
        diag = jj == ii
        qk_cols = slice(kq * HEAD, (kq + 1) * HEAD)
        cs = [cg * par + cc for cc in range(par)]
        rows = [pl.ds(pl.multiple_of(c * c_len, c_len), c_len) for c in cs]
        ks = [k_ref[r, qk_cols] for r in rows]
        qs = [q_ref[r, qk_cols] for r in rows]
        kks = [_dot_nt(k, k) for k in ks]
        qks = [_dot_nt(q, k) for q, k in zip(qs, ks)]
        kfs = [k.astype(F32) for k in ks]
        kts = [kf.T for kf in kfs]
        chains = [(cc, kq * rep + e) for cc in range(par) for e in range(rep)]
        g_rows = [g_ref[hv, cs[cc]] for cc, hv in chains]
        b_rows = [beta_ref[hv, cs[cc]] for cc, hv in chains]
        gc_cols = [jnp.sum(jnp.where(causal, g, 0.0), axis=1, keepdims=True) for g in g_rows]
        gc_rows = [jnp.sum(jnp.where(diag, gc, 0.0), axis=0, keepdims=True) for gc in gc_cols]
        b_cols = [jnp.sum(jnp.where(diag, b, 0.0), axis=1, keepdims=True) for b in b_rows]
        g_lasts = [jnp.sum(g, axis=1, keepdims=True) for g in g_rows]
        decays = [jnp.where(causal, jnp.exp(gc - gr), 0.0) for gc, gr in zip(gc_cols, gc_rows)]
        lows = [jnp.where(strict, bc * kks[cc] * dec, 0.0)
                for (cc, hv), bc, dec in zip(chains, b_cols, decays)]
        t_invs = _unit_lower_inverses(lows, ii, jj)
        for n, (cc, hv) in enumerate(chains):
            c = cs[cc]
            e_gc = jnp.exp(gc_cols[n])
            v = v_ref[rows[cc], hv * HEAD:(hv + 1) * HEAD].astype(F32)
            kb = kfs[cc] * b_cols[n]
            rhs = jnp.concatenate([v * b_cols[n], kb * e_gc], axis=1).astype(BF16)
            sol = _dot(t_invs[n].astype(BF16), rhs)
            u_ref[c, hv] = sol[:, :HEAD]
            wq_ref[c, hv, 0:c_len, :] = sol[:, HEAD:].astype(BF16)
            wq_ref[c, hv, c_len:2 * c_len, :] = (qs[cc].astype(F32) * e_gc).astype(BF16)
            intra_ref[c, hv] = (qks[cc] * decays[n]).astype(BF16)
            kdt_ref[c, hv] = (kts[cc] * jnp.exp(g_lasts[n] - gc_rows[n])).astype(BF16)

    for kq in range(k_heads):
        def local_body(cg, carry, kq=kq):
            chunk_local(kq, cg)
            return carry
        lax.fori_loop(0, n_chunks // par, local_body, 0)

    gain = gain_ref[...]
    heads = range(k_heads * rep)

    def chunk_recurrent(c, carry):
        rows = pl.ds(pl.multiple_of(c * c_len, c_len), c_len)
        states = [state_ref[hv] for hv in heads]
        ws_qs = [_dot(wq_ref[c, hv], states[hv].astype(BF16)) for hv in heads]
        v_news = [(u_ref[c, hv] - ws_qs[hv][:c_len]).astype(BF16) for hv in heads]
        upds = [_dot(kdt_ref[c, hv], v_news[hv]) for hv in heads]
        for hv in heads:
            decay_all = jnp.exp(jnp.sum(g_ref[hv, c], axis=1, keepdims=True))
            state_ref[hv] = states[hv] * decay_all + upds[hv]
        for hv in heads:
            o = ws_qs[hv][c_len:] + _dot(intra_ref[c, hv], v_news[hv])
            z = z_ref[rows, hv * HEAD:(hv + 1) * HEAD].astype(F32)
            o = _rms_rows(o) * gain * (z * jax.nn.sigmoid(z))
            o_ref[rows, hv * HEAD:(hv + 1) * HEAD] = o.astype(o_ref.dtype)
        return carry

    lax.fori_loop(0, n_chunks, chunk_recurrent, 0)


def _gdn_delta(qkv, proj, gates, out_gain, qk_heads, v_heads, z_col0, tt=1024, par=8, k_heads=4):
    s = qkv.shape[0]
    rep = v_heads // qk_heads
    c_len = DELTA_CHUNK
    n_chunks = tt // c_len
    g4 = gates.reshape(LANES, s // c_len, 1, c_len)
    qw = k_heads * HEAD
    vw = k_heads * rep * HEAD
    nh = k_heads * rep
    return pl.pallas_call(
        functools.partial(_gdn_delta_kernel, n_chunks=n_chunks, k_heads=k_heads, rep=rep, par=par),
        grid=(qk_heads // k_heads, s // tt),
        in_specs=[
            pl.BlockSpec((tt, qw), lambda h, t: (t, h)),
            pl.BlockSpec((tt, qw), lambda h, t: (t, qk_heads // k_heads + h)),
            pl.BlockSpec((tt, vw), lambda h, t: (t, (2 * qk_heads * HEAD) // vw + h)),
            pl.BlockSpec((tt, vw), lambda h, t: (t, z_col0 // vw + h)),
            pl.BlockSpec((nh, n_chunks, 1, c_len), lambda h, t: (h, t, 0, 0)),
            pl.BlockSpec((nh, n_chunks, 1, c_len), lambda h, t: (v_heads // nh + h, t, 0, 0)),
            pl.BlockSpec((1, HEAD), lambda h, t: (0, 0)),
        ],
        out_specs=pl.BlockSpec((tt, vw), lambda h, t: (t, h)),
        out_shape=jax.ShapeDtypeStruct((s, v_heads * HEAD), BF16),
        scratch_shapes=[pltpu.VMEM((nh, HEAD, HEAD), F32),
                        pltpu.VMEM((n_chunks, nh, c_len, HEAD), F32),
                        pltpu.VMEM((n_chunks, nh, 2 * c_len, HEAD), BF16),
                        pltpu.VMEM((n_chunks, nh, c_len, c_len), BF16),
                        pltpu.VMEM((n_chunks, nh, HEAD, c_len), BF16)],
        compiler_params=_cparams("parallel", "arbitrary"),
        name="gdn_delta",
    )(qkv, qkv, qkv, proj, g4, g4, out_gain.reshape(1, HEAD))


def _pad_cols(w, width):
    return jnp.pad(w, ((0, 0), (0, width - w.shape[1])))


def _mixer_a(h, gain, w_in, w_out, slot, lam_q1, lam_k1, lam_q2, lam_k2, sub_norm, lambda_init):
    d = h.shape[1]
    heads = d // (2 * HEAD)
    qk_cols = 2 * heads * HEAD
    colscale = jnp.concatenate([jnp.full((qk_cols,), LOG2_E * HEAD ** -0.5, F32),
                                jnp.ones((w_in.shape[2] - qk_cols,), F32)])
    qkv = _norm_proj(h, gain, w_in, slot, colscale=colscale)
    lam_params = jnp.stack([lam_q1, lam_k1, lam_q2, lam_k2]).astype(F32)
    o = _diff_attention(qkv, lam_params, sub_norm, lambda_init, heads)
    return _out_proj(o, w_out, slot, h)


def _mixer_b(h, gain, w_in, w_in_tail, w_out, slot, forget_bias, q_norm, k_norm):
    d = h.shape[1]
    heads = d // HEAD
    width = heads * HEAD
    colscale = jnp.concatenate([jnp.tile(q_norm, heads) * (LOG2_E * HEAD ** -0.5), jnp.tile(k_norm, heads),
                                jnp.ones((2 * width,), F32)])
    qkvg = _norm_proj(h, gain, w_in, slot, n_cols=4 * width, colscale=colscale, norm_cols=2 * width)
    f_logit = _norm_proj(h, gain, _pad_cols(w_in_tail, LANES).astype(BF16)[None], 0, out_dtype=F32)
    bias = jnp.pad(forget_bias, (0, LANES - heads)).reshape(1, LANES)
    cum_t = _fox_cum_log_forget(f_logit, bias)[:heads].reshape(heads, 1, -1)
    o = _fox_attention(qkvg, cum_t, heads)
    return _out_proj(o, w_out, slot, h)


def _mixer_c(h, gain, w_in, w_in_tail, w_out, slot, conv_w, a_log, dt_bias, out_norm):
    d = h.shape[1]
    qk_heads = d // HEAD
    v_heads = 2 * qk_heads
    qk_w = qk_heads * HEAD
    v_w = v_heads * HEAD
    main = 2 * qk_w + 2 * v_w
    proj = _gdn_norm_proj_conv(h, gain, w_in, slot, conv_w, n_cols=main, qk_cols=2 * qk_w,
                               q_cols=qk_w)
    ba = _norm_proj(h, gain, _pad_cols(w_in_tail, LANES).astype(BF16)[None], 0, out_dtype=F32)
    gates = _gdn_gates(ba, a_log, dt_bias, v_heads)
    o = _gdn_delta(proj, proj, gates, out_norm, qk_heads, v_heads, z_col0=2 * qk_w + v_w)
    return _out_proj(o, w_out, slot, h)


def kernel(x, mix_norm, ffn_norm, final_norm, a_w_in, a_w_out, a_lam_q1, a_lam_k1, a_lam_q2, a_lam_k2,
           a_sub_norm, b_w_in, b_w_out, b_forget_bias, b_q_norm, b_k_norm, c_w_in, c_w_out, c_conv_w,
           c_a_log, c_dt_bias, c_out_norm, ffn_w_gate, ffn_w_up, ffn_w_down):
    batch, seq, d = x.shape
    depth = mix_norm.shape[0]
    a_in, a_out = a_w_in.astype(BF16), a_w_out.astype(BF16)
    b_in, b_out = b_w_in.astype(BF16), b_w_out.astype(BF16)
    c_in, c_out = c_w_in.astype(BF16), c_w_out.astype(BF16)
    w_gate, w_up, w_down = ffn_w_gate.astype(BF16), ffn_w_up.astype(BF16), ffn_w_down.astype(BF16)
    b_main = 4 * d
    c_main = 6 * d
    outs = []
    for b in range(batch):
        h = x.reshape(seq, d) if batch == 1 else x[b]
        for i in range(depth):
            slot = i // N_MIXERS
            if i % N_MIXERS == 0:
                lambda_init = 0.8 - 0.6 * math.exp(-0.3 * i)
                h = _mixer_a(h, mix_norm[i], a_in, a_out, slot, a_lam_q1[slot], a_lam_k1[slot],
                             a_lam_q2[slot], a_lam_k2[slot], a_sub_norm[slot], lambda_init)
            elif i % N_MIXERS == 1:
                h = _mixer_b(h, mix_norm[i], b_in, b_w_in[slot][:, b_main:], b_out, slot, b_forget_bias[slot],
                             b_q_norm[slot], b_k_norm[slot])
            else:
                h = _mixer_c(h, mix_norm[i], c_in, c_w_in[slot][:, c_main:], c_out, slot, c_conv_w[slot],
                             c_a_log[slot], c_dt_bias[slot], c_out_norm[slot])
            h = _ffn(h, ffn_norm[i], w_gate, w_up, w_down, i,
                     final_gain=final_norm if i == depth - 1 else None)
        outs.append(h)
    return outs[0].reshape(1, seq, d) if batch == 1 else jnp.stack(outs)
```

```python
import functools
import math

import jax
import jax.numpy as jnp
from jax import lax
from jax.experimental import pallas as pl
from jax.experimental.pallas import tpu as pltpu

F32 = jnp.float32
BF16 = jnp.bfloat16
I32 = jnp.int32

EPS = 1e-6
MASK_CHUNK = 64
HEAD = 128
LANES = 128
N_MIXERS = 3
LOG2_E = math.log2(math.e)
DELTA_CHUNK = 128
VMEM_LIMIT_BYTES = 52 * 1024 * 1024
FFN_VMEM_LIMIT_BYTES = 58 * 1024 * 1024


def _cparams(*sem, vmem_limit_bytes=VMEM_LIMIT_BYTES):
    return pltpu.CompilerParams(dimension_semantics=sem, vmem_limit_bytes=vmem_limit_bytes)


def _dot(a, b):
    return jnp.dot(a, b, preferred_element_type=F32)


def _dot_nt(a, b):
    return lax.dot_general(a, b, (((1,), (1,)), ((), ())), preferred_element_type=F32)


def _rms_rows(x):
    return x * lax.rsqrt(jnp.mean(x * x, axis=-1, keepdims=True) + EPS)


def _proj_kernel(*refs, tn, norm_blocks, has_colscale):
    if has_colscale:
        x_ref, g_ref, w_ref, cs_ref, o_ref, xn_ref = refs
    else:
        x_ref, g_ref, w_ref, o_ref, xn_ref = refs
        cs_ref = None
    n = pl.program_id(1)

    @pl.when(n == 0)
    def _():
        xn_ref[...] = (_rms_rows(x_ref[...]) * g_ref[...]).astype(BF16)

    acc = _dot(xn_ref[...], w_ref[...])

    def finish(a):
        if cs_ref is not None:
            a = a * cs_ref[...]
        o_ref[...] = a.astype(o_ref.dtype)

    if norm_blocks == 0:
        finish(acc)
    else:
        @pl.when(n < norm_blocks)
        def _():
            parts = [_rms_rows(acc[:, c * HEAD:(c + 1) * HEAD]) for c in range(tn // HEAD)]
            finish(jnp.concatenate(parts, axis=1))

        @pl.when(n >= norm_blocks)
        def _():
            finish(acc)


def _norm_proj(h, gain, w, layer, n_cols=None, colscale=None, norm_cols=0, out_dtype=BF16, tm=1024, tn=1024):
    s, d = h.shape
    n = w.shape[2] if n_cols is None else n_cols
    tn = min(tn, n)
    assert s % tm == 0 and n % tn == 0 and norm_cols % tn == 0 and tn % HEAD == 0
    in_specs = [
        pl.BlockSpec((tm, d), lambda m, j: (m, 0)),
        pl.BlockSpec((1, d), lambda m, j: (0, 0)),
        pl.BlockSpec((None, d, tn), lambda m, j: (layer, 0, j)),
    ]
    args = [h, gain.reshape(1, d), w]
    if colscale is not None:
        in_specs.append(pl.BlockSpec((1, tn), lambda m, j: (0, j)))
        args.append(colscale.reshape(1, n))
    return pl.pallas_call(
        functools.partial(_proj_kernel, tn=tn, norm_blocks=norm_cols // tn,
                          has_colscale=colscale is not None),
        grid=(s // tm, n // tn),
        in_specs=in_specs,
        out_specs=pl.BlockSpec((tm, tn), lambda m, j: (m, j)),
        out_shape=jax.ShapeDtypeStruct((s, n), out_dtype),
        scratch_shapes=[pltpu.VMEM((tm, d), BF16)],
        compiler_params=_cparams("parallel", "arbitrary"),
        name="norm_proj",
    )(*args)


def _out_proj_kernel(o_ref, w_ref, h_ref, y_ref):
    y_ref[...] = h_ref[...] + _dot(o_ref[...], w_ref[...])


def _out_proj(o, w, layer, h, tm=1024, tn=None):
    s, k = o.shape
    d = w.shape[2]
    if tn is None:
        tn = 1024 if k <= 2048 else 512
    return pl.pallas_call(
        _out_proj_kernel,
        grid=(s // tm, d // tn),
        in_specs=[
            pl.BlockSpec((tm, k), lambda m, j: (m, 0)),
            pl.BlockSpec((None, k, tn), lambda m, j: (layer, 0, j)),
            pl.BlockSpec((tm, tn), lambda m, j: (m, j)),
        ],
        out_specs=pl.BlockSpec((tm, tn), lambda m, j: (m, j)),
        out_shape=jax.ShapeDtypeStruct((s, d), F32),
        compiler_params=_cparams("parallel", "arbitrary"),
        name="out_proj",
    )(o, w, h)


def _ffn_kernel(*refs, has_final):
    if has_final:
        h_ref, g_ref, wg_ref, wu_ref, wd_ref, fg_ref, y_ref, xn_ref = refs
    else:
        h_ref, g_ref, wg_ref, wu_ref, wd_ref, y_ref, xn_ref = refs
        fg_ref = None
    f = pl.program_id(1)

    @pl.when(f == 0)
    def _():
        h = h_ref[...]
        xn_ref[...] = (_rms_rows(h) * g_ref[...]).astype(BF16)
        y_ref[...] = h

    xn = xn_ref[...]
    gate = _dot(xn, wg_ref[...])
    up = _dot(xn, wu_ref[...])
    act = (gate * jax.nn.sigmoid(gate) * up).astype(BF16)
    y_ref[...] += _dot(act, wd_ref[...])

    if fg_ref is not None:
        @pl.when(f == pl.num_programs(1) - 1)
        def _():
            y_ref[...] = _rms_rows(y_ref[...]) * fg_ref[...]


def _ffn(h, gain, w_gate, w_up, w_down, layer, final_gain=None, tm=1024, tf=512):
    s, d = h.shape
    dff = w_gate.shape[2]
    in_specs = [
        pl.BlockSpec((tm, d), lambda m, f: (m, 0)),
        pl.BlockSpec((1, d), lambda m, f: (0, 0)),
        pl.BlockSpec((None, d, tf), lambda m, f: (layer, 0, f)),
        pl.BlockSpec((None, d, tf), lambda m, f: (layer, 0, f)),
        pl.BlockSpec((None, tf, d), lambda m, f: (layer, f, 0)),
    ]
    args = [h, gain.reshape(1, d), w_gate, w_up, w_down]
    if final_gain is not None:
        in_specs.append(pl.BlockSpec((1, d), lambda m, f: (0, 0)))
        args.append(final_gain.reshape(1, d))
    return pl.pallas_call(
        functools.partial(_ffn_kernel, has_final=final_gain is not None),
        grid=(s // tm, dff // tf),
        in_specs=in_specs,
        out_specs=pl.BlockSpec((tm, d), lambda m, f: (m, 0)),
        out_shape=jax.ShapeDtypeStruct((s, d), F32),
        scratch_shapes=[pltpu.VMEM((tm, d), BF16)],
        compiler_params=_cparams("parallel", "arbitrary", vmem_limit_bytes=FFN_VMEM_LIMIT_BYTES),
        name="swiglu",
    )(*args)


def _kv_block_start(t, q0, tk):
    return pl.multiple_of(jnp.where(t < 2, q0 + t * tk, (t - 2) * tk), tk)


def _run_kv_pipeline(n_full_pairs, scores, softmax, weighted_values):
    scores(0, 0, True)
    scores(1, 1, True)
    softmax(0)

    def body(u, carry):
        t = 2 * u
        scores(t + 2, 0, False)
        softmax(1)
        weighted_values(t, 0)
        scores(t + 3, 1, False)
        softmax(0)
        weighted_values(t + 1, 1)
        return carry

    lax.fori_loop(0, n_full_pairs, body, 0)
    n = 2 + 2 * n_full_pairs
    softmax(1)
    weighted_values(n - 2, 0)
    weighted_values(n - 1, 1)


def _diff_attn_kernel(q1_ref, q2_ref, k1_ref, k2_ref, v_ref, lam_ref, sn_ref, o_ref,
                      s_ref, p_ref, a_ref, m_ref, l_ref, acc_ref, *, tq, tk, lambda_init):
    i = pl.program_id(1)
    q0 = pl.multiple_of(i * tq, tq)
    lane_tiles = tk // LANES
    v_tiles = v_ref.shape[1] // LANES
    m_ref[...] = jnp.full(m_ref.shape, -jnp.inf, F32)
    l_ref[...] = jnp.zeros(l_ref.shape, F32)
    acc_ref[...] = jnp.zeros(acc_ref.shape, F32)
    qs = (q1_ref[...], q2_ref[...])
    ks = (k1_ref, k2_ref)

    def scores(t, slot, masked):
        k0 = _kv_block_start(t, q0, tk)
        if masked:
            row = lax.broadcasted_iota(I32, (tq, tk), 0)
            col = lax.broadcasted_iota(I32, (tq, tk), 1) + (k0 - q0)
            visible = (col // MASK_CHUNK) <= (row // MASK_CHUNK)
        for mp in range(2):
            s = _dot_nt(qs[mp], ks[mp][pl.ds(k0, tk), :])
            if masked:
                s = jnp.where(visible, s, -jnp.inf)
            s_ref[slot, mp] = s

    def softmax(slot):
        for mp in range(2):
            s = s_ref[slot, mp]
            m_prev = m_ref[mp]
            m_new = jnp.maximum(m_prev, jnp.max(s, axis=-1, keepdims=True))
            alpha = jnp.exp2(m_prev - m_new)
            l_new = alpha * l_ref[mp]
            for c in range(lane_tiles):
                p = jnp.exp2(s[:, c * LANES:(c + 1) * LANES] - m_new)
                l_new = l_new + p
                p_ref[slot, mp, :, c * LANES:(c + 1) * LANES] = p.astype(BF16)
            l_ref[mp] = l_new
            m_ref[mp] = m_new
            a_ref[slot, mp] = alpha

    def weighted_values(t, slot):
        k0 = _kv_block_start(t, q0, tk)
        v = v_ref[pl.ds(k0, tk), :]
        for mp in range(2):
            alpha = jnp.concatenate([a_ref[slot, mp]] * v_tiles, axis=1)
            acc_ref[mp] = alpha * acc_ref[mp] + _dot(p_ref[slot, mp], v)

    _run_kv_pipeline(i * (tq // (2 * tk)), scores, softmax, weighted_values)

    lam = (jnp.exp(jnp.sum(lam_ref[0:1, :] * lam_ref[1:2, :], keepdims=True))
           - jnp.exp(jnp.sum(lam_ref[2:3, :] * lam_ref[3:4, :], keepdims=True)) + lambda_init)
    l1 = jnp.sum(l_ref[0], axis=-1, keepdims=True)
    l2 = jnp.sum(l_ref[1], axis=-1, keepdims=True)
    o = acc_ref[0] / l1 - lam * (acc_ref[1] / l2)
    o = _rms_rows(o) * sn_ref[...] * (1.0 - lambda_init)
    o_ref[...] = o.astype(o_ref.dtype)


def _diff_attention(qkv, lam_params, sub_norm, lambda_init, heads, tq=512):
    s = qkv.shape[0]
    vd = 2 * HEAD
    tk = tq // 2
    hb = heads
    return pl.pallas_call(
        functools.partial(_diff_attn_kernel, tq=tq, tk=tk, lambda_init=lambda_init),
        grid=(heads, s // tq),
        in_specs=[
            pl.BlockSpec((tq, HEAD), lambda h, i: (i, h)),
            pl.BlockSpec((tq, HEAD), lambda h, i: (i, hb + h)),
            pl.BlockSpec((s, HEAD), lambda h, i: (0, 2 * hb + h)),
            pl.BlockSpec((s, HEAD), lambda h, i: (0, 3 * hb + h)),
            pl.BlockSpec((s, vd), lambda h, i: (0, 2 * hb + h)),
            pl.BlockSpec((4, HEAD), lambda h, i: (0, 0)),
            pl.BlockSpec((1, vd), lambda h, i: (0, 0)),
        ],
        out_specs=pl.BlockSpec((tq, vd), lambda h, i: (i, h)),
        out_shape=jax.ShapeDtypeStruct((s, heads * vd), BF16),
        scratch_shapes=[pltpu.VMEM((2, 2, tq, tk), F32), pltpu.VMEM((2, 2, tq, tk), BF16),
                        pltpu.VMEM((2, 2, tq, LANES), F32), pltpu.VMEM((2, tq, LANES), F32),
                        pltpu.VMEM((2, tq, LANES), F32), pltpu.VMEM((2, tq, vd), F32)],
        compiler_params=_cparams("parallel", "arbitrary"),
        name="diff_attention",
    )(qkv, qkv, qkv, qkv, qkv, lam_params, sub_norm.reshape(1, vd))


def _fox_gate_kernel(fl_ref, b_ref, o_ref, carry_ref, *, tb):
    @pl.when(pl.program_id(0) == 0)
    def _():
        carry_ref[...] = jnp.zeros(carry_ref.shape, F32)

    log_f = jax.nn.log_sigmoid(fl_ref[...] + b_ref[...])
    tri = (lax.broadcasted_iota(I32, (tb, tb), 1) <= lax.broadcasted_iota(I32, (tb, tb), 0)).astype(F32)
    cum = jnp.dot(tri, log_f, preferred_element_type=F32,
                  precision=lax.Precision.HIGHEST) + carry_ref[...]
    carry_ref[...] = cum[tb - 1:tb, :]
    o_ref[...] = cum.T


def _fox_cum_log_forget(f_logit, bias, tb=256):
    s = f_logit.shape[0]
    return pl.pallas_call(
        functools.partial(_fox_gate_kernel, tb=tb),
        grid=(s // tb,),
        in_specs=[pl.BlockSpec((tb, LANES), lambda i: (i, 0)),
                  pl.BlockSpec((1, LANES), lambda i: (0, 0))],
        out_specs=pl.BlockSpec((LANES, tb), lambda i: (0, i)),
        out_shape=jax.ShapeDtypeStruct((LANES, s), F32),
        scratch_shapes=[pltpu.VMEM((1, LANES), F32)],
        compiler_params=_cparams("arbitrary"),
        name="fox_cum_log_forget",
    )(f_logit, bias)


def _fox_attn_kernel(q_ref, k_ref, v_ref, gate_ref, ck_ref, o_ref,
                     s_ref, p_ref, a_ref, m_ref, l_ref, acc_ref, *, tq, tk):
    i = pl.program_id(1)
    q0 = pl.multiple_of(i * tq, tq)
    lane_tiles = tk // LANES
    m_ref[...] = jnp.full(m_ref.shape, -jnp.inf, F32)
    l_ref[...] = jnp.zeros(l_ref.shape, F32)
    acc_ref[...] = jnp.zeros(acc_ref.shape, F32)
    q = q_ref[...]
    base = ck_ref[:, pl.ds(q0, LANES)][:, 0:1]

    def scores(t, slot, masked):
        k0 = _kv_block_start(t, q0, tk)
        s = _dot_nt(q, k_ref[pl.ds(k0, tk), :]) + (base - ck_ref[:, pl.ds(k0, tk)]) * LOG2_E
        if masked:
            row = lax.broadcasted_iota(I32, (tq, tk), 0)
            col = lax.broadcasted_iota(I32, (tq, tk), 1)
            s = jnp.where(col + (k0 - q0) <= row, s, -jnp.inf)
        s_ref[slot] = s

    def softmax(slot):
        s = s_ref[slot]
        m_prev = m_ref[...]
        m_new = jnp.maximum(m_prev, jnp.max(s, axis=-1, keepdims=True))
        alpha = jnp.exp2(m_prev - m_new)
        l_new = alpha * l_ref[...]
        for c in range(lane_tiles):
            p = jnp.exp2(s[:, c * LANES:(c + 1) * LANES] - m_new)
            l_new = l_new + p
            p_ref[slot, :, c * LANES:(c + 1) * LANES] = p.astype(BF16)
        l_ref[...] = l_new
        m_ref[...] = m_new
        a_ref[slot] = alpha

    def weighted_values(t, slot):
        k0 = _kv_block_start(t, q0, tk)
        acc_ref[...] = a_ref[slot] * acc_ref[...] + _dot(p_ref[slot], v_ref[pl.ds(k0, tk), :])

    _run_kv_pipeline(i * (tq // (2 * tk)), scores, softmax, weighted_values)

    l = jnp.sum(l_ref[...], axis=-1, keepdims=True)
    o = acc_ref[...] / l * jax.nn.sigmoid(gate_ref[...].astype(F32))
    o_ref[...] = o.astype(o_ref.dtype)


def _fox_attention(qkvg, cum_t, heads, tq=1024):
    s = qkvg.shape[0]
    tk = tq // 2
    return pl.pallas_call(
        functools.partial(_fox_attn_kernel, tq=tq, tk=tk),
        grid=(heads, s // tq),
        in_specs=[
            pl.BlockSpec((tq, HEAD), lambda h, i: (i, h)),
            pl.BlockSpec((s, HEAD), lambda h, i: (0, heads + h)),
            pl.BlockSpec((s, HEAD), lambda h, i: (0, 2 * heads + h)),
            pl.BlockSpec((tq, HEAD), lambda h, i: (i, 3 * heads + h)),
            pl.BlockSpec((None, 1, s), lambda h, i: (h, 0, 0)),
        ],
        out_specs=pl.BlockSpec((tq, HEAD), lambda h, i: (i, h)),
        out_shape=jax.ShapeDtypeStruct((s, heads * HEAD), BF16),
        scratch_shapes=[pltpu.VMEM((2, tq, tk), F32), pltpu.VMEM((2, tq, tk), BF16),
                        pltpu.VMEM((2, tq, LANES), F32), pltpu.VMEM((tq, LANES), F32),
                        pltpu.VMEM((tq, LANES), F32), pltpu.VMEM((tq, HEAD), F32)],
        compiler_params=_cparams("parallel", "arbitrary"),
        name="fox_attention",
    )(qkvg, qkvg, qkvg, qkvg, cum_t)


def _gdn_proj_kernel(x_ref, g_ref, w_ref, cw_ref, o_ref, xn_ref, halo_ref, *,
                     tn, q_blocks, qk_blocks, conv_blocks):
    m = pl.program_id(0)
    n = pl.program_id(1)

    @pl.when(n == 0)
    def _():
        xn_ref[...] = (_rms_rows(x_ref[...]) * g_ref[...]).astype(BF16)

    acc = _dot(xn_ref[...], w_ref[...])
    tm = acc.shape[0]

    def conv_silu():
        w = cw_ref[...]
        taps = w.shape[0]
        halo = jnp.where(m > 0, halo_ref[n], 0.0)
        halo_ref[n] = acc[tm - 8:tm, :]
        x_top = acc[0:8, :]
        row8 = lax.broadcasted_iota(I32, (8, tn), 0)
        y = acc * w[taps - 1:taps, :]
        y_top = x_top * w[taps - 1:taps, :]
        for sh in range(1, taps):
            wj = w[taps - 1 - sh:taps - sh, :]
            y = y + pltpu.roll(acc, sh, 0) * wj
            before = pltpu.roll(halo, sh, 0)
            y_top = y_top + jnp.where(row8 < sh, before, pltpu.roll(x_top, sh, 0)) * wj
        return y * jax.nn.sigmoid(y), y_top * jax.nn.sigmoid(y_top)

    def emit(transform):
        y, y_top = conv_silu()
        o_ref[...] = transform(y).astype(o_ref.dtype)
        o_ref[0:8, :] = transform(y_top).astype(o_ref.dtype)

    def l2_heads(scale):
        def transform(a):
            parts = []
            for c in range(tn // HEAD):
                b = a[:, c * HEAD:(c + 1) * HEAD]
                parts.append(b * (lax.rsqrt(jnp.sum(b * b, axis=-1, keepdims=True) + EPS) * scale))
            return jnp.concatenate(parts, axis=1)
        return transform

    @pl.when(n < q_blocks)
    def _():
        emit(l2_heads(HEAD ** -0.5))

    @pl.when((n >= q_blocks) & (n < qk_blocks))
    def _():
        emit(l2_heads(1.0))

    @pl.when((n >= qk_blocks) & (n < conv_blocks))
    def _():
        emit(lambda a: a)

    @pl.when(n >= conv_blocks)
    def _():
        o_ref[...] = acc.astype(o_ref.dtype)


def _gdn_norm_proj_conv(h, gain, w, layer, conv_w, n_cols, qk_cols, q_cols, tm=1024, tn=1024):
    s, d = h.shape
    taps, c = conv_w.shape
    assert s % tm == 0 and n_cols % tn == 0 and c % tn == 0 and qk_cols % tn == 0 and q_cols % tn == 0
    conv_blocks = c // tn
    return pl.pallas_call(
        functools.partial(_gdn_proj_kernel, tn=tn, q_blocks=q_cols // tn, qk_blocks=qk_cols // tn,
                          conv_blocks=conv_blocks),
        grid=(s // tm, n_cols // tn),
        in_specs=[
            pl.BlockSpec((tm, d), lambda m, j: (m, 0)),
            pl.BlockSpec((1, d), lambda m, j: (0, 0)),
            pl.BlockSpec((None, d, tn), lambda m, j: (layer, 0, j)),
            pl.BlockSpec((taps, tn), lambda m, j: (0, jnp.minimum(j, conv_blocks - 1))),
        ],
        out_specs=pl.BlockSpec((tm, tn), lambda m, j: (m, j)),
        out_shape=jax.ShapeDtypeStruct((s, n_cols), BF16),
        scratch_shapes=[pltpu.VMEM((tm, d), BF16), pltpu.VMEM((conv_blocks, 8, tn), F32)],
        compiler_params=_cparams("arbitrary", "arbitrary"),
        name="gdn_norm_proj_conv",
    )(h, gain.reshape(1, d), w, conv_w)


def _gdn_gate_kernel(x_ref, alog_ref, dtb_ref, o_ref, *, v_heads):
    x = x_ref[...]
    beta = jax.nn.sigmoid(x)
    g = -jnp.exp(alog_ref[...]) * jax.nn.softplus(x + dtb_ref[...])
    lane = lax.broadcasted_iota(I32, x.shape, 1)
    o_ref[...] = jnp.where(lane < v_heads, beta, g).T


def _gdn_gates(ba, a_log, dt_bias, v_heads, tb=512):
    s = ba.shape[0]
    pad = LANES - 2 * v_heads
    alog = jnp.concatenate([jnp.zeros((v_heads,), F32), a_log, jnp.zeros((pad,), F32)]).reshape(1, LANES)
    dtb = jnp.concatenate([jnp.zeros((v_heads,), F32), dt_bias, jnp.zeros((pad,), F32)]).reshape(1, LANES)
    return pl.pallas_call(
        functools.partial(_gdn_gate_kernel, v_heads=v_heads),
        grid=(s // tb,),
        in_specs=[pl.BlockSpec((tb, LANES), lambda i: (i, 0)),
                  pl.BlockSpec((1, LANES), lambda i: (0, 0)),
                  pl.BlockSpec((1, LANES), lambda i: (0, 0))],
        out_specs=pl.BlockSpec((LANES, tb), lambda i: (0, i)),
        out_shape=jax.ShapeDtypeStruct((LANES, s), F32),
        compiler_params=_cparams("parallel"),
        name="gdn_gates",
    )(ba, alog, dtb)


def _unit_lower_inverses(lows, ii, jj):
    c = lows[0].shape[0]
    xor = ii ^ jj
    eye = (ii == jj).astype(F32)
    xs = [eye - jnp.where(xor < 2, low, 0.0) for low in lows]
    b = 2
    while b < c:
        lower_left = (xor >= b) & (xor < 2 * b)
        offs = [jnp.where(lower_left, low, 0.0).astype(BF16) for low in lows]
        xbs = [x.astype(BF16) for x in xs]
        ys = [_dot(xb, off).astype(BF16) for xb, off in zip(xbs, offs)]
        xs = [x - _dot(y, xb) for x, y, xb in zip(xs, ys, xbs)]
        b *= 2
    return xs


def _gdn_delta_kernel(q_ref, k_ref, v_ref, z_ref, beta_ref, g_ref, gain_ref, o_ref,
                      state_ref, u_ref, wq_ref, intra_ref, kdt_ref, *, n_chunks, k_heads, rep, par):
    c_len = DELTA_CHUNK

    @pl.when(pl.program_id(1) == 0)
    def _():
        state_ref[...] = jnp.zeros(state_ref.shape, F32)

    def chunk_local(kq, cg):
        ii = lax.broadcasted_iota(I32, (c_len, c_len), 0)
        jj = lax.broadcasted_iota(I32, (c_len, c_len), 1)
        causal = jj <= ii
        strict = jj < ii
        diag = jj == ii
        qk_cols = slice(kq * HEAD, (kq + 1) * HEAD)
        cs = [cg * par + cc for cc in range(par)]
        rows = [pl.ds(pl.multiple_of(c * c_len, c_len), c_len) for c in cs]
        ks = [k_ref[r, qk_cols] for r in rows]
        qs = [q_ref[r, qk_cols] for r in rows]
        kks = [_dot_nt(k, k) for k in ks]
        qks = [_dot_nt(q, k) for q, k in zip(qs, ks)]
        kfs = [k.astype(F32) for k in ks]
        kts = [kf.T for kf in kfs]
        chains = [(cc, kq * rep + e) for cc in range(par) for e in range(rep)]
        g_rows = [g_ref[hv, cs[cc]] for cc, hv in chains]
        b_rows = [beta_ref[hv, cs[cc]] for cc, hv in chains]
        gc_cols = [jnp.sum(jnp.where(causal, g, 0.0), axis=1, keepdims=True) for g in g_rows]
        gc_rows = [jnp.sum(jnp.where(diag, gc, 0.0), axis=0, keepdims=True) for gc in gc_cols]
        b_cols = [jnp.sum(jnp.where(diag, b, 0.0), axis=1, keepdims=True) for b in b_rows]
        g_lasts = [jnp.sum(g, axis=1, keepdims=True) for g in g_rows]
        decays = [jnp.where(causal, jnp.exp(gc - gr), 0.0) for gc, gr in zip(gc_cols, gc_rows)]
        lows = [jnp.where(strict, bc * kks[cc] * dec, 0.0)
                for (cc, hv), bc, dec in zip(chains, b_cols, decays)]
        t_invs = _unit_lower_inverses(lows, ii, jj)
        for n, (cc, hv) in enumerate(chains):
            c = cs[cc]
            e_gc = jnp.exp(gc_cols[n])
            v = v_ref[rows[cc], hv * HEAD:(hv + 1) * HEAD].astype(F32)
            kb = kfs[cc] * b_cols[n]
            rhs = jnp.concatenate([v * b_cols[n], kb * e_gc], axis=1).astype(BF16)
            sol = _dot(t_invs[n].astype(BF16), rhs)
            u_ref[c, hv] = sol[:, :HEAD]
            wq_ref[c, hv, 0:c_len, :] = sol[:, HEAD:].astype(BF16)
            wq_ref[c, hv, c_len:2 * c_len, :] = (qs[cc].astype(F32) * e_gc).astype(BF16)
            intra_ref[c, hv] = (qks[cc] * decays[n]).astype(BF16)
            kdt_ref[c, hv] = (kts[cc] * jnp.exp(g_lasts[n] - gc_rows[n])).astype(BF16)

    for kq in range(k_heads):
        def local_body(cg, carry, kq=kq):
            chunk_local(kq, cg)
            return carry
        lax.fori_loop(0, n_chunks // par, local_body, 0)

    gain = gain_ref[...]
    heads = range(k_heads * rep)

    def chunk_recurrent(c, carry):
        rows = pl.ds(pl.multiple_of(c * c_len, c_len), c_len)
        states = [state_ref[hv] for hv in heads]
        ws_qs = [_dot(wq_ref[c, hv], states[hv].astype(BF16)) for hv in heads]
        v_news = [(u_ref[c, hv] - ws_qs[hv][:c_len]).astype(BF16) for hv in heads]
        upds = [_dot(kdt_ref[c, hv], v_news[hv]) for hv in heads]
        for hv in heads:
            decay_all = jnp.exp(jnp.sum(g_ref[hv, c], axis=1, keepdims=True))
            state_ref[hv] = states[hv] * decay_all + upds[hv]
        for hv in heads:
            o = ws_qs[hv][c_len:] + _dot(intra_ref[c, hv], v_news[hv])
            z = z_ref[rows, hv * HEAD:(hv + 1) * HEAD].astype(F32)
            o = _rms_rows(o) * gain * (z * jax.nn.sigmoid(z))
            o_ref[rows, hv * HEAD:(hv + 1) * HEAD] = o.astype(o_ref.dtype)
        return carry

    lax.fori_loop(0, n_chunks, chunk_recurrent, 0)


def _gdn_delta(qkv, proj, gates, out_gain, qk_heads, v_heads, z_col0, tt=1024, par=8, k_heads=4):
    s = qkv.shape[0]
    rep = v_heads // qk_heads
    c_len = DELTA_CHUNK
    n_chunks = tt // c_len
    g4 = gates.reshape(LANES, s // c_len, 1, c_len)
    qw = k_heads * HEAD
    vw = k_heads * rep * HEAD
    nh = k_heads * rep
    return pl.pallas_call(
        functools.partial(_gdn_delta_kernel, n_chunks=n_chunks, k_heads=k_heads, rep=rep, par=par),
        grid=(qk_heads // k_heads, s // tt),
        in_specs=[
            pl.BlockSpec((tt, qw), lambda h, t: (t, h)),
            pl.BlockSpec((tt, qw), lambda h, t: (t, qk_heads // k_heads + h)),
            pl.BlockSpec((tt, vw), lambda h, t: (t, (2 * qk_heads * HEAD) // vw + h)),
            pl.BlockSpec((tt, vw), lambda h, t: (t, z_col0 // vw + h)),
            pl.BlockSpec((nh, n_chunks, 1, c_len), lambda h, t: (h, t, 0, 0)),
            pl.BlockSpec((nh, n_chunks, 1, c_len), lambda h, t: (v_heads // nh + h, t, 0, 0)),
            pl.BlockSpec((1, HEAD), lambda h, t: (0, 0)),
        ],
        out_specs=pl.BlockSpec((tt, vw), lambda h, t: (t, h)),
        out_shape=jax.ShapeDtypeStruct((s, v_heads * HEAD), BF16),
        scratch_shapes=[pltpu.VMEM((nh, HEAD, HEAD), F32),
                        pltpu.VMEM((n_chunks, nh, c_len, HEAD), F32),
                        pltpu.VMEM((n_chunks, nh, 2 * c_len, HEAD), BF16),
                        pltpu.VMEM((n_chunks, nh, c_len, c_len), BF16),
                        pltpu.VMEM((n_chunks, nh, HEAD, c_len), BF16)],
        compiler_params=_cparams("parallel", "arbitrary"),
        name="gdn_delta",
    )(qkv, qkv, qkv, proj, g4, g4, out_gain.reshape(1, HEAD))


def _pad_cols(w, width):
    return jnp.pad(w, ((0, 0), (0, width - w.shape[1])))


def _mixer_a(h, gain, w_in, w_out, slot, lam_q1, lam_k1, lam_q2, lam_k2, sub_norm, lambda_init):
    d = h.shape[1]
    heads = d // (2 * HEAD)
    qk_cols = 2 * heads * HEAD
    colscale = jnp.concatenate([jnp.full((qk_cols,), LOG2_E * HEAD ** -0.5, F32),
                                jnp.ones((w_in.shape[2] - qk_cols,), F32)])
    qkv = _norm_proj(h, gain, w_in, slot, colscale=colscale)
    lam_params = jnp.stack([lam_q1, lam_k1, lam_q2, lam_k2]).astype(F32)
    o = _diff_attention(qkv, lam_params, sub_norm, lambda_init, heads)
    return _out_proj(o, w_out, slot, h)


def _mixer_b(h, gain, w_in, w_in_tail, w_out, slot, forget_bias, q_norm, k_norm):
    d = h.shape[1]
    heads = d // HEAD
    width = heads * HEAD
    colscale = jnp.concatenate([jnp.tile(q_norm, heads) * (LOG2_E * HEAD ** -0.5), jnp.tile(k_norm, heads),
                                jnp.ones((2 * width,), F32)])
    qkvg = _norm_proj(h, gain, w_in, slot, n_cols=4 * width, colscale=colscale, norm_cols=2 * width)
    f_logit = _norm_proj(h, gain, _pad_cols(w_in_tail, LANES).astype(BF16)[None], 0, out_dtype=F32)
    bias = jnp.pad(forget_bias, (0, LANES - heads)).reshape(1, LANES)
    cum_t = _fox_cum_log_forget(f_logit, bias)[:heads].reshape(heads, 1, -1)
    o = _fox_attention(qkvg, cum_t, heads)
    return _out_proj(o, w_out, slot, h)


def _mixer_c(h, gain, w_in, w_in_tail, w_out, slot, conv_w, a_log, dt_bias, out_norm):
    d = h.shape[1]
    qk_heads = d // HEAD
    v_heads = 2 * qk_heads
    qk_w = qk_heads * HEAD
    v_w = v_heads * HEAD
    main = 2 * qk_w + 2 * v_w
    proj = _gdn_norm_proj_conv(h, gain, w_in, slot, conv_w, n_cols=main, qk_cols=2 * qk_w,
                               q_cols=qk_w)
    ba = _norm_proj(h, gain, _pad_cols(w_in_tail, LANES).astype(BF16)[None], 0, out_dtype=F32)
    gates = _gdn_gates(ba, a_log, dt_bias, v_heads)
    o = _gdn_delta(proj, proj, gates, out_norm, qk_heads, v_heads, z_col0=2 * qk_w + v_w)
    return _out_proj(o, w_out, slot, h)


def kernel(x, mix_norm, ffn_norm, final_norm, a_w_in, a_w_out, a_lam_q1, a_lam_k1, a_lam_q2, a_lam_k2,
           a_sub_norm, b_w_in, b_w_out, b_forget_bias, b_q_norm, b_k_norm, c_w_in, c_w_out, c_conv_w,
           c_a_log, c_dt_bias, c_out_norm, ffn_w_gate, ffn_w_up, ffn_w_down):
    batch, seq, d = x.shape
    depth = mix_norm.shape[0]
    a_in, a_out = a_w_in.astype(BF16), a_w_out.astype(BF16)
    b_in, b_out = b_w_in.astype(BF16), b_w_out.astype(BF16)
    c_in, c_out = c_w_in.astype(BF16), c_w_out.astype(BF16)
    w_gate, w_up, w_down = ffn_w_gate.astype(BF16), ffn_w_up.astype(BF16), ffn_w_down.astype(BF16)
    b_main = 4 * d
    c_main = 6 * d
    outs = []
    for b in range(batch):
        h = x.reshape(seq, d) if batch == 1 else x[b]
        for i in range(depth):
            slot = i // N_MIXERS
            if i % N_MIXERS == 0:
                lambda_init = 0.8 - 0.6 * math.exp(-0.3 * i)
                h = _mixer_a(h, mix_norm[i], a_in, a_out, slot, a_lam_q1[slot], a_lam_k1[slot],
                             a_lam_q2[slot], a_lam_k2[slot], a_sub_norm[slot], lambda_init)
            elif i % N_MIXERS == 1:
                h = _mixer_b(h, mix_norm[i], b_in, b_w_in[slot][:, b_main:], b_out, slot, b_forget_bias[slot],
                             b_q_norm[slot], b_k_norm[slot])
            else:
                h = _mixer_c(h, mix_norm[i], c_in, c_w_in[slot][:, c_main:], c_out, slot, c_conv_w[slot],
                             c_a_log[slot], c_dt_bias[slot], c_out_norm[slot])
            h = _ffn(h, ffn_norm[i], w_gate, w_up, w_down, i,
                     final_gain=final_norm if i == depth - 1 else None)
        outs.append(h)
    return outs[0].reshape(1, seq, d) if batch == 1 else jnp.stack(outs)
```

```python
import functools
import math

import jax
import jax.numpy as jnp
from jax import lax
from jax.experimental import pallas as pl
from jax.experimental.pallas import tpu as pltpu

F32 = jnp.float32
BF16 = jnp.bfloat16
I32 = jnp.int32

EPS = 1e-6
MASK_CHUNK = 64
HEAD = 128
LANES = 128
N_MIXERS = 3
LOG2_E = math.log2(math.e)
DELTA_CHUNK = 128
VMEM_LIMIT_BYTES = 52 * 1024 * 1024
FFN_VMEM_LIMIT_BYTES = 58 * 1024 * 1024


def _cparams(*sem, vmem_limit_bytes=VMEM_LIMIT_BYTES):
    return pltpu.CompilerParams(dimension_semantics=sem, vmem_limit_bytes=vmem_limit_bytes)


def _dot(a, b):
    return jnp.dot(a, b, preferred_element_type=F32)


def _dot_nt(a, b):
    return lax.dot_general(a, b, (((1,), (1,)), ((), ())), preferred_element_type=F32)


def _rms_rows(x):
    return x * lax.rsqrt(jnp.mean(x * x, axis=-1, keepdims=True) + EPS)


def _proj_kernel(*refs, tn, norm_blocks, has_colscale):
    if has_colscale:
        x_ref, g_ref, w_ref, cs_ref, o_ref, xn_ref = refs
    else:
        x_ref, g_ref, w_ref, o_ref, xn_ref = refs
        cs_ref = None
    n = pl.program_id(1)

    @pl.when(n == 0)
    def _():
        xn_ref[...] = (_rms_rows(x_ref[...]) * g_ref[...]).astype(BF16)

    acc = _dot(xn_ref[...], w_ref[...])

    def finish(a):
        if cs_ref is not None:
            a = a * cs_ref[...]
        o_ref[...] = a.astype(o_ref.dtype)

    if norm_blocks == 0:
        finish(acc)
    else:
        @pl.when(n < norm_blocks)
        def _():
            parts = [_rms_rows(acc[:, c * HEAD:(c + 1) * HEAD]) for c in range(tn // HEAD)]
            finish(jnp.concatenate(parts, axis=1))

        @pl.when(n >= norm_blocks)
        def _():
            finish(acc)


def _norm_proj(h, gain, w, layer, n_cols=None, colscale=None, norm_cols=0, out_dtype=BF16, tm=1024, tn=2048):
    s, d = h.shape
    n = w.shape[2] if n_cols is None else n_cols
    tn = min(tn, n)
    assert s % tm == 0 and n % tn == 0 and norm_cols % tn == 0 and tn % HEAD == 0
    in_specs = [
        pl.BlockSpec((tm, d), lambda m, j: (m, 0)),
        pl.BlockSpec((1, d), lambda m, j: (0, 0)),
        pl.BlockSpec((None, d, tn), lambda m, j: (layer, 0, j)),
    ]
    args = [h, gain.reshape(1, d), w]
    if colscale is not None:
        in_specs.append(pl.BlockSpec((1, tn), lambda m, j: (0, j)))
        args.append(colscale.reshape(1, n))
    return pl.pallas_call(
        functools.partial(_proj_kernel, tn=tn, norm_blocks=norm_cols // tn,
                          has_colscale=colscale is not None),
        grid=(s // tm, n // tn),
        in_specs=in_specs,
        out_specs=pl.BlockSpec((tm, tn), lambda m, j: (m, j)),
        out_shape=jax.ShapeDtypeStruct((s, n), out_dtype),
        scratch_shapes=[pltpu.VMEM((tm, d), BF16)],
        compiler_params=_cparams("parallel", "arbitrary", vmem_limit_bytes=FFN_VMEM_LIMIT_BYTES),
        name="norm_proj",
    )(*args)


def _out_proj_kernel(o_ref, w_ref, h_ref, y_ref):
    y_ref[...] = h_ref[...] + _dot(o_ref[...], w_ref[...])


def _out_proj(o, w, layer, h, tm=1024, tn=None):
    s, k = o.shape
    d = w.shape[2]
    if tn is None:
        tn = 1024 if k <= 2048 else 512
    return pl.pallas_call(
        _out_proj_kernel,
        grid=(s // tm, d // tn),
        in_specs=[
            pl.BlockSpec((tm, k), lambda m, j: (m, 0)),
            pl.BlockSpec((None, k, tn), lambda m, j: (layer, 0, j)),
            pl.BlockSpec((tm, tn), lambda m, j: (m, j)),
        ],
        out_specs=pl.BlockSpec((tm, tn), lambda m, j: (m, j)),
        out_shape=jax.ShapeDtypeStruct((s, d), F32),
        compiler_params=_cparams("parallel", "arbitrary"),
        name="out_proj",
    )(o, w, h)


def _ffn_kernel(*refs, has_final):
    if has_final:
        h_ref, g_ref, wg_ref, wu_ref, wd_ref, fg_ref, y_ref, xn_ref = refs
    else:
        h_ref, g_ref, wg_ref, wu_ref, wd_ref, y_ref, xn_ref = refs
        fg_ref = None
    f = pl.program_id(1)

    @pl.when(f == 0)
    def _():
        h = h_ref[...]
        xn_ref[...] = (_rms_rows(h) * g_ref[...]).astype(BF16)
        y_ref[...] = h

    xn = xn_ref[...]
    gate = _dot(xn, wg_ref[...])
    up = _dot(xn, wu_ref[...])
    act = (gate * jax.nn.sigmoid(gate) * up).astype(BF16)
    y_ref[...] += _dot(act, wd_ref[...])

    if fg_ref is not None:
        @pl.when(f == pl.num_programs(1) - 1)
        def _():
            y_ref[...] = _rms_rows(y_ref[...]) * fg_ref[...]


def _ffn(h, gain, w_gate, w_up, w_down, layer, final_gain=None, tm=1024, tf=512):
    s, d = h.shape
    dff = w_gate.shape[2]
    in_specs = [
        pl.BlockSpec((tm, d), lambda m, f: (m, 0)),
        pl.BlockSpec((1, d), lambda m, f: (0, 0)),
        pl.BlockSpec((None, d, tf), lambda m, f: (layer, 0, f)),
        pl.BlockSpec((None, d, tf), lambda m, f: (layer, 0, f)),
        pl.BlockSpec((None, tf, d), lambda m, f: (layer, f, 0)),
    ]
    args = [h, gain.reshape(1, d), w_gate, w_up, w_down]
    if final_gain is not None:
        in_specs.append(pl.BlockSpec((1, d), lambda m, f: (0, 0)))
        args.append(final_gain.reshape(1, d))
    return pl.pallas_call(
        functools.partial(_ffn_kernel, has_final=final_gain is not None),
        grid=(s // tm, dff // tf),
        in_specs=in_specs,
        out_specs=pl.BlockSpec((tm, d), lambda m, f: (m, 0)),
        out_shape=jax.ShapeDtypeStruct((s, d), F32),
        scratch_shapes=[pltpu.VMEM((tm, d), BF16)],
        compiler_params=_cparams("parallel", "arbitrary", vmem_limit_bytes=FFN_VMEM_LIMIT_BYTES),
        name="swiglu",
    )(*args)


def _kv_block_start(t, q0, tk):
    return pl.multiple_of(jnp.where(t < 2, q0 + t * tk, (t - 2) * tk), tk)


def _run_kv_pipeline(n_full_pairs, scores, softmax, weighted_values):
    scores(0, 0, True)
    scores(1, 1, True)
    softmax(0)

    def body(u, carry):
        t = 2 * u
        scores(t + 2, 0, False)
        softmax(1)
        weighted_values(t, 0)
        scores(t + 3, 1, False)
        softmax(0)
        weighted_values(t + 1, 1)
        return carry

    lax.fori_loop(0, n_full_pairs, body, 0)
    n = 2 + 2 * n_full_pairs
    softmax(1)
    weighted_values(n - 2, 0)
    weighted_values(n - 1, 1)


def _diff_attn_kernel(q1_ref, q2_ref, k1_ref, k2_ref, v_ref, lam_ref, sn_ref, o_ref,
                      s_ref, p_ref, a_ref, m_ref, l_ref, acc_ref, *, tq, tk, lambda_init):
    i = pl.program_id(1)
    q0 = pl.multiple_of(i * tq, tq)
    lane_tiles = tk // LANES
    v_tiles = v_ref.shape[1] // LANES
    m_ref[...] = jnp.full(m_ref.shape, -jnp.inf, F32)
    l_ref[...] = jnp.zeros(l_ref.shape, F32)
    acc_ref[...] = jnp.zeros(acc_ref.shape, F32)
    qs = (q1_ref[...], q2_ref[...])
    ks = (k1_ref, k2_ref)

    def scores(t, slot, masked):
        k0 = _kv_block_start(t, q0, tk)
        if masked:
            row = lax.broadcasted_iota(I32, (tq, tk), 0)
            col = lax.broadcasted_iota(I32, (tq, tk), 1) + (k0 - q0)
            visible = (col // MASK_CHUNK) <= (row // MASK_CHUNK)
        for mp in range(2):
            s = _dot_nt(qs[mp], ks[mp][pl.ds(k0, tk), :])
            if masked:
                s = jnp.where(visible, s, -jnp.inf)
            s_ref[slot, mp] = s

    def softmax(slot):
        for mp in range(2):
            s = s_ref[slot, mp]
            m_prev = m_ref[mp]
            m_new = jnp.maximum(m_prev, jnp.max(s, axis=-1, keepdims=True))
            alpha = jnp.exp2(m_prev - m_new)
            l_new = alpha * l_ref[mp]
            for c in range(lane_tiles):
                p = jnp.exp2(s[:, c * LANES:(c + 1) * LANES] - m_new)
                l_new = l_new + p
                p_ref[slot, mp, :, c * LANES:(c + 1) * LANES] = p.astype(BF16)
            l_ref[mp] = l_new
            m_ref[mp] = m_new
            a_ref[slot, mp] = alpha

    def weighted_values(t, slot):
        k0 = _kv_block_start(t, q0, tk)
        v = v_ref[pl.ds(k0, tk), :]
        for mp in range(2):
            alpha = jnp.concatenate([a_ref[slot, mp]] * v_tiles, axis=1)
            acc_ref[mp] = alpha * acc_ref[mp] + _dot(p_ref[slot, mp], v)

    _run_kv_pipeline(i * (tq // (2 * tk)), scores, softmax, weighted_values)

    lam = (jnp.exp(jnp.sum(lam_ref[0:1, :] * lam_ref[1:2, :], keepdims=True))
           - jnp.exp(jnp.sum(lam_ref[2:3, :] * lam_ref[3:4, :], keepdims=True)) + lambda_init)
    l1 = jnp.sum(l_ref[0], axis=-1, keepdims=True)
    l2 = jnp.sum(l_ref[1], axis=-1, keepdims=True)
    o = acc_ref[0] / l1 - lam * (acc_ref[1] / l2)
    o = _rms_rows(o) * sn_ref[...] * (1.0 - lambda_init)
    o_ref[...] = o.astype(o_ref.dtype)


def _diff_attention(qkv, lam_params, sub_norm, lambda_init, heads, tq=512):
    s = qkv.shape[0]
    vd = 2 * HEAD
    tk = tq // 2
    hb = heads
    return pl.pallas_call(
        functools.partial(_diff_attn_kernel, tq=tq, tk=tk, lambda_init=lambda_init),
        grid=(heads, s // tq),
        in_specs=[
            pl.BlockSpec((tq, HEAD), lambda h, i: (i, h)),
            pl.BlockSpec((tq, HEAD), lambda h, i: (i, hb + h)),
            pl.BlockSpec((s, HEAD), lambda h, i: (0, 2 * hb + h)),
            pl.BlockSpec((s, HEAD), lambda h, i: (0, 3 * hb + h)),
            pl.BlockSpec((s, vd), lambda h, i: (0, 2 * hb + h)),
            pl.BlockSpec((4, HEAD), lambda h, i: (0, 0)),
            pl.BlockSpec((1, vd), lambda h, i: (0, 0)),
        ],
        out_specs=pl.BlockSpec((tq, vd), lambda h, i: (i, h)),
        out_shape=jax.ShapeDtypeStruct((s, heads * vd), BF16),
        scratch_shapes=[pltpu.VMEM((2, 2, tq, tk), F32), pltpu.VMEM((2, 2, tq, tk), BF16),
                        pltpu.VMEM((2, 2, tq, LANES), F32), pltpu.VMEM((2, tq, LANES), F32),
                        pltpu.VMEM((2, tq, LANES), F32), pltpu.VMEM((2, tq, vd), F32)],
        compiler_params=_cparams("parallel", "arbitrary"),
        name="diff_attention",
    )(qkv, qkv, qkv, qkv, qkv, lam_params, sub_norm.reshape(1, vd))


def _fox_gate_kernel(fl_ref, b_ref, o_ref, carry_ref, *, tb):
    @pl.when(pl.program_id(0) == 0)
    def _():
        carry_ref[...] = jnp.zeros(carry_ref.shape, F32)

    log_f = jax.nn.log_sigmoid(fl_ref[...] + b_ref[...])
    tri = (lax.broadcasted_iota(I32, (tb, tb), 1) <= lax.broadcasted_iota(I32, (tb, tb), 0)).astype(F32)
    cum = jnp.dot(tri, log_f, preferred_element_type=F32,
                  precision=lax.Precision.HIGHEST) + carry_ref[...]
    carry_ref[...] = cum[tb - 1:tb, :]
    o_ref[...] = cum.T


def _fox_cum_log_forget(f_logit, bias, tb=256):
    s = f_logit.shape[0]
    return pl.pallas_call(
        functools.partial(_fox_gate_kernel, tb=tb),
        grid=(s // tb,),
        in_specs=[pl.BlockSpec((tb, LANES), lambda i: (i, 0)),
                  pl.BlockSpec((1, LANES), lambda i: (0, 0))],
        out_specs=pl.BlockSpec((LANES, tb), lambda i: (0, i)),
        out_shape=jax.ShapeDtypeStruct((LANES, s), F32),
        scratch_shapes=[pltpu.VMEM((1, LANES), F32)],
        compiler_params=_cparams("arbitrary"),
        name="fox_cum_log_forget",
    )(f_logit, bias)


def _fox_attn_kernel(q_ref, k_ref, v_ref, gate_ref, ck_ref, o_ref,
                     s_ref, p_ref, a_ref, m_ref, l_ref, acc_ref, *, tq, tk):
    i = pl.program_id(1)
    q0 = pl.multiple_of(i * tq, tq)
    lane_tiles = tk // LANES
    m_ref[...] = jnp.full(m_ref.shape, -jnp.inf, F32)
    l_ref[...] = jnp.zeros(l_ref.shape, F32)
    acc_ref[...] = jnp.zeros(acc_ref.shape, F32)
    q = q_ref[...]
    base = ck_ref[:, pl.ds(q0, LANES)][:, 0:1]

    def scores(t, slot, masked):
        k0 = _kv_block_start(t, q0, tk)
        s = _dot_nt(q, k_ref[pl.ds(k0, tk), :]) + (base - ck_ref[:, pl.ds(k0, tk)]) * LOG2_E
        if masked:
            row = lax.broadcasted_iota(I32, (tq, tk), 0)
            col = lax.broadcasted_iota(I32, (tq, tk), 1)
            s = jnp.where(col + (k0 - q0) <= row, s, -jnp.inf)
        s_ref[slot] = s

    def softmax(slot):
        s = s_ref[slot]
        m_prev = m_ref[...]
        m_new = jnp.maximum(m_prev, jnp.max(s, axis=-1, keepdims=True))
        alpha = jnp.exp2(m_prev - m_new)
        l_new = alpha * l_ref[...]
        for c in range(lane_tiles):
            p = jnp.exp2(s[:, c * LANES:(c + 1) * LANES] - m_new)
            l_new = l_new + p
            p_ref[slot, :, c * LANES:(c + 1) * LANES] = p.astype(BF16)
        l_ref[...] = l_new
        m_ref[...] = m_new
        a_ref[slot] = alpha

    def weighted_values(t, slot):
        k0 = _kv_block_start(t, q0, tk)
        acc_ref[...] = a_ref[slot] * acc_ref[...] + _dot(p_ref[slot], v_ref[pl.ds(k0, tk), :])

    _run_kv_pipeline(i * (tq // (2 * tk)), scores, softmax, weighted_values)

    l = jnp.sum(l_ref[...], axis=-1, keepdims=True)
    o = acc_ref[...] / l * jax.nn.sigmoid(gate_ref[...].astype(F32))
    o_ref[...] = o.astype(o_ref.dtype)


def _fox_attention(qkvg, cum_t, heads, tq=1024):
    s = qkvg.shape[0]
    tk = tq // 2
    return pl.pallas_call(
        functools.partial(_fox_attn_kernel, tq=tq, tk=tk),
        grid=(heads, s // tq),
        in_specs=[
            pl.BlockSpec((tq, HEAD), lambda h, i: (i, h)),
            pl.BlockSpec((s, HEAD), lambda h, i: (0, heads + h)),
            pl.BlockSpec((s, HEAD), lambda h, i: (0, 2 * heads + h)),
            pl.BlockSpec((tq, HEAD), lambda h, i: (i, 3 * heads + h)),
            pl.BlockSpec((None, 1, s), lambda h, i: (h, 0, 0)),
        ],
        out_specs=pl.BlockSpec((tq, HEAD), lambda h, i: (i, h)),
        out_shape=jax.ShapeDtypeStruct((s, heads * HEAD), BF16),
        scratch_shapes=[pltpu.VMEM((2, tq, tk), F32), pltpu.VMEM((2, tq, tk), BF16),
                        pltpu.VMEM((2, tq, LANES), F32), pltpu.VMEM((tq, LANES), F32),
                        pltpu.VMEM((tq, LANES), F32), pltpu.VMEM((tq, HEAD), F32)],
        compiler_params=_cparams("parallel", "arbitrary"),
        name="fox_attention",
    )(qkvg, qkvg, qkvg, qkvg, cum_t)


def _gdn_proj_kernel(x_ref, g_ref, w_ref, cw_ref, o_ref, xn_ref, halo_ref, *,
                     tn, q_blocks, qk_blocks, conv_blocks):
    m = pl.program_id(0)
    n = pl.program_id(1)

    @pl.when(n == 0)
    def _():
        xn_ref[...] = (_rms_rows(x_ref[...]) * g_ref[...]).astype(BF16)

    acc = _dot(xn_ref[...], w_ref[...])
    tm = acc.shape[0]

    def conv_silu():
        w = cw_ref[...]
        taps = w.shape[0]
        halo = jnp.where(m > 0, halo_ref[n], 0.0)
        halo_ref[n] = acc[tm - 8:tm, :]
        x_top = acc[0:8, :]
        row8 = lax.broadcasted_iota(I32, (8, tn), 0)
        y = acc * w[taps - 1:taps, :]
        y_top = x_top * w[taps - 1:taps, :]
        for sh in range(1, taps):
            wj = w[taps - 1 - sh:taps - sh, :]
            y = y + pltpu.roll(acc, sh, 0) * wj
            before = pltpu.roll(halo, sh, 0)
            y_top = y_top + jnp.where(row8 < sh, before, pltpu.roll(x_top, sh, 0)) * wj
        return y * jax.nn.sigmoid(y), y_top * jax.nn.sigmoid(y_top)

    def emit(transform):
        y, y_top = conv_silu()
        o_ref[...] = transform(y).astype(o_ref.dtype)
        o_ref[0:8, :] = transform(y_top).astype(o_ref.dtype)

    def l2_heads(scale):
        def transform(a):
            parts = []
            for c in range(tn // HEAD):
                b = a[:, c * HEAD:(c + 1) * HEAD]
                parts.append(b * (lax.rsqrt(jnp.sum(b * b, axis=-1, keepdims=True) + EPS) * scale))
            return jnp.concatenate(parts, axis=1)
        return transform

    @pl.when(n < q_blocks)
    def _():
        emit(l2_heads(HEAD ** -0.5))

    @pl.when((n >= q_blocks) & (n < qk_blocks))
    def _():
        emit(l2_heads(1.0))

    @pl.when((n >= qk_blocks) & (n < conv_blocks))
    def _():
        emit(lambda a: a)

    @pl.when(n >= conv_blocks)
    def _():
        o_ref[...] = acc.astype(o_ref.dtype)


def _gdn_norm_proj_conv(h, gain, w, layer, conv_w, n_cols, qk_cols, q_cols, tm=1024, tn=1024):
    s, d = h.shape
    taps, c = conv_w.shape
    assert s % tm == 0 and n_cols % tn == 0 and c % tn == 0 and qk_cols % tn == 0 and q_cols % tn == 0
    conv_blocks = c // tn
    return pl.pallas_call(
        functools.partial(_gdn_proj_kernel, tn=tn, q_blocks=q_cols // tn, qk_blocks=qk_cols // tn,
                          conv_blocks=conv_blocks),
        grid=(s // tm, n_cols // tn),
        in_specs=[
            pl.BlockSpec((tm, d), lambda m, j: (m, 0)),
            pl.BlockSpec((1, d), lambda m, j: (0, 0)),
            pl.BlockSpec((None, d, tn), lambda m, j: (layer, 0, j)),
            pl.BlockSpec((taps, tn), lambda m, j: (0, jnp.minimum(j, conv_blocks - 1))),
        ],
        out_specs=pl.BlockSpec((tm, tn), lambda m, j: (m, j)),
        out_shape=jax.ShapeDtypeStruct((s, n_cols), BF16),
        scratch_shapes=[pltpu.VMEM((tm, d), BF16), pltpu.VMEM((conv_blocks, 8, tn), F32)],
        compiler_params=_cparams("arbitrary", "arbitrary"),
        name="gdn_norm_proj_conv",
    )(h, gain.reshape(1, d), w, conv_w)


def _gdn_gate_kernel(x_ref, alog_ref, dtb_ref, o_ref, *, v_heads):
    x = x_ref[...]
    beta = jax.nn.sigmoid(x)
    g = -jnp.exp(alog_ref[...]) * jax.nn.softplus(x + dtb_ref[...])
    lane = lax.broadcasted_iota(I32, x.shape, 1)
    o_ref[...] = jnp.where(lane < v_heads, beta, g).T


def _gdn_gates(ba, a_log, dt_bias, v_heads, tb=512):
    s = ba.shape[0]
    pad = LANES - 2 * v_heads
    alog = jnp.concatenate([jnp.zeros((v_heads,), F32), a_log, jnp.zeros((pad,), F32)]).reshape(1, LANES)
    dtb = jnp.concatenate([jnp.zeros((v_heads,), F32), dt_bias, jnp.zeros((pad,), F32)]).reshape(1, LANES)
    return pl.pallas_call(
        functools.partial(_gdn_gate_kernel, v_heads=v_heads),
        grid=(s // tb,),
        in_specs=[pl.BlockSpec((tb, LANES), lambda i: (i, 0)),
                  pl.BlockSpec((1, LANES), lambda i: (0, 0)),
                  pl.BlockSpec((1, LANES), lambda i: (0, 0))],
        out_specs=pl.BlockSpec((LANES, tb), lambda i: (0, i)),
        out_shape=jax.ShapeDtypeStruct((LANES, s), F32),
        compiler_params=_cparams("parallel"),
        name="gdn_gates",
    )(ba, alog, dtb)


def _unit_lower_inverses(lows, ii, jj):
    c = lows[0].shape[0]
    xor = ii ^ jj
    eye = (ii == jj).astype(F32)
    xs = [eye - jnp.where(xor < 2, low, 0.0) for low in lows]
    b = 2
    while b < c:
        lower_left = (xor >= b) & (xor < 2 * b)
        offs = [jnp.where(lower_left, low, 0.0).astype(BF16) for low in lows]
        xbs = [x.astype(BF16) for x in xs]
        ys = [_dot(xb, off).astype(BF16) for xb, off in zip(xbs, offs)]
        xs = [x - _dot(y, xb) for x, y, xb in zip(xs, ys, xbs)]
        b *= 2
    return xs


def _gdn_delta_kernel(q_ref, k_ref, v_ref, z_ref, beta_ref, g_ref, gain_ref, o_ref,
                      state_ref, u_ref, wq_ref, intra_ref, kdt_ref, *, n_chunks, k_heads, rep, par):
    c_len = DELTA_CHUNK

    @pl.when(pl.program_id(1) == 0)
    def _():
        state_ref[...] = jnp.zeros(state_ref.shape, F32)

    def chunk_local(kq, cg):
        ii = lax.broadcasted_iota(I32, (c_len, c_len), 0)
        jj = lax.broadcasted_iota(I32, (c_len, c_len), 1)
        causal = jj <= ii
        strict = jj < ii
        diag = jj == ii
        qk_cols = slice(kq * HEAD, (kq + 1) * HEAD)
        cs = [cg * par + cc for cc in range(par)]
        rows = [pl.ds(pl.multiple_of(c * c_len, c_len), c_len) for c in cs]
        ks = [k_ref[r, qk_cols] for r in rows]
        qs = [q_ref[r, qk_cols] for r in rows]
        kks = [_dot_nt(k, k) for k in ks]
        qks = [_dot_nt(q, k) for q, k in zip(qs, ks)]
        kfs = [k.astype(F32) for k in ks]
        kts = [kf.T for kf in kfs]
        chains = [(cc, kq * rep + e) for cc in range(par) for e in range(rep)]
        g_rows = [g_ref[hv, cs[cc]] for cc, hv in chains]
        b_rows = [beta_ref[hv, cs[cc]] for cc, hv in chains]
        gc_cols = [jnp.sum(jnp.where(causal, g, 0.0), axis=1, keepdims=True) for g in g_rows]
        gc_rows = [jnp.sum(jnp.where(diag, gc, 0.0), axis=0, keepdims=True) for gc in gc_cols]
        b_cols = [jnp.sum(jnp.where(diag, b, 0.0), axis=1, keepdims=True) for b in b_rows]
        g_lasts = [jnp.sum(g, axis=1, keepdims=True) for g in g_rows]
        decays = [jnp.where(causal, jnp.exp(gc - gr), 0.0) for gc, gr in zip(gc_cols, gc_rows)]
        lows = [jnp.where(strict, bc * kks[cc] * dec, 0.0)
                for (cc, hv), bc, dec in zip(chains, b_cols, decays)]
        t_invs = _unit_lower_inverses(lows, ii, jj)
        for n, (cc, hv) in enumerate(chains):
            c = cs[cc]
            e_gc = jnp.exp(gc_cols[n])
            v = v_ref[rows[cc], hv * HEAD:(hv + 1) * HEAD].astype(F32)
            kb = kfs[cc] * b_cols[n]
            rhs = jnp.concatenate([v * b_cols[n], kb * e_gc], axis=1).astype(BF16)
            sol = _dot(t_invs[n].astype(BF16), rhs)
            u_ref[c, hv] = sol[:, :HEAD]
            wq_ref[c, hv, 0:c_len, :] = sol[:, HEAD:].astype(BF16)
            wq_ref[c, hv, c_len:2 * c_len, :] = (qs[cc].astype(F32) * e_gc).astype(BF16)
            intra_ref[c, hv] = (qks[cc] * decays[n]).astype(BF16)
            kdt_ref[c, hv] = (kts[cc] * jnp.exp(g_lasts[n] - gc_rows[n])).astype(BF16)

    for kq in range(k_heads):
        def local_body(cg, carry, kq=kq):
            chunk_local(kq, cg)
            return carry
        lax.fori_loop(0, n_chunks // par, local_body, 0)

    gain = gain_ref[...]
    heads = range(k_heads * rep)

    def chunk_recurrent(c, carry):
        rows = pl.ds(pl.multiple_of(c * c_len, c_len), c_len)
        states = [state_ref[hv] for hv in heads]
        ws_qs = [_dot(wq_ref[c, hv], states[hv].astype(BF16)) for hv in heads]
        v_news = [(u_ref[c, hv] - ws_qs[hv][:c_len]).astype(BF16) for hv in heads]
        upds = [_dot(kdt_ref[c, hv], v_news[hv]) for hv in heads]
        for hv in heads:
            decay_all = jnp.exp(jnp.sum(g_ref[hv, c], axis=1, keepdims=True))
            state_ref[hv] = states[hv] * decay_all + upds[hv]
        for hv in heads:
            o = ws_qs[hv][c_len:] + _dot(intra_ref[c, hv], v_news[hv])
            z = z_ref[rows, hv * HEAD:(hv + 1) * HEAD].astype(F32)
            o = _rms_rows(o) * gain * (z * jax.nn.sigmoid(z))
            o_ref[rows, hv * HEAD:(hv + 1) * HEAD] = o.astype(o_ref.dtype)
        return carry

    lax.fori_loop(0, n_chunks, chunk_recurrent, 0)


def _gdn_delta(qkv, proj, gates, out_gain, qk_heads, v_heads, z_col0, tt=1024, par=8, k_heads=4):
    s = qkv.shape[0]
    rep = v_heads // qk_heads
    c_len = DELTA_CHUNK
    n_chunks = tt // c_len
    g4 = gates.reshape(LANES, s // c_len, 1, c_len)
    qw = k_heads * HEAD
    vw = k_heads * rep * HEAD
    nh = k_heads * rep
    return pl.pallas_call(
        functools.partial(_gdn_delta_kernel, n_chunks=n_chunks, k_heads=k_heads, rep=rep, par=par),
        grid=(qk_heads // k_heads, s // tt),
        in_specs=[
            pl.BlockSpec((tt, qw), lambda h, t: (t, h)),
            pl.BlockSpec((tt, qw), lambda h, t: (t, qk_heads // k_heads + h)),
            pl.BlockSpec((tt, vw), lambda h, t: (t, (2 * qk_heads * HEAD) // vw + h)),
            pl.BlockSpec((tt, vw), lambda h, t: (t, z_col0 // vw + h)),
            pl.BlockSpec((nh, n_chunks, 1, c_len), lambda h, t: (h, t, 0, 0)),
            pl.BlockSpec((nh, n_chunks, 1, c_len), lambda h, t: (v_heads // nh + h, t, 0, 0)),
            pl.BlockSpec((1, HEAD), lambda h, t: (0, 0)),
        ],
        out_specs=pl.BlockSpec((tt, vw), lambda h, t: (t, h)),
        out_shape=jax.ShapeDtypeStruct((s, v_heads * HEAD), BF16),
        scratch_shapes=[pltpu.VMEM((nh, HEAD, HEAD), F32),
                        pltpu.VMEM((n_chunks, nh, c_len, HEAD), F32),
                        pltpu.VMEM((n_chunks, nh, 2 * c_len, HEAD), BF16),
                        pltpu.VMEM((n_chunks, nh, c_len, c_len), BF16),
                        pltpu.VMEM((n_chunks, nh, HEAD, c_len), BF16)],
        compiler_params=_cparams("parallel", "arbitrary"),
        name="gdn_delta",
    )(qkv, qkv, qkv, proj, g4, g4, out_gain.reshape(1, HEAD))


def _pad_cols(w, width):
    return jnp.pad(w, ((0, 0), (0, width - w.shape[1])))


def _mixer_a(h, gain, w_in, w_out, slot, lam_q1, lam_k1, lam_q2, lam_k2, sub_norm, lambda_init):
    d = h.shape[1]
    heads = d // (2 * HEAD)
    qk_cols = 2 * heads * HEAD
    colscale = jnp.concatenate([jnp.full((qk_cols,), LOG2_E * HEAD ** -0.5, F32),
                                jnp.ones((w_in.shape[2] - qk_cols,), F32)])
    qkv = _norm_proj(h, gain, w_in, slot, colscale=colscale)
    lam_params = jnp.stack([lam_q1, lam_k1, lam_q2, lam_k2]).astype(F32)
    o = _diff_attention(qkv, lam_params, sub_norm, lambda_init, heads)
    return _out_proj(o, w_out, slot, h)


def _mixer_b(h, gain, w_in, w_in_tail, w_out, slot, forget_bias, q_norm, k_norm):
    d = h.shape[1]
    heads = d // HEAD
    width = heads * HEAD
    colscale = jnp.concatenate([jnp.tile(q_norm, heads) * (LOG2_E * HEAD ** -0.5), jnp.tile(k_norm, heads),
                                jnp.ones((2 * width,), F32)])
    qkvg = _norm_proj(h, gain, w_in, slot, n_cols=4 * width, colscale=colscale, norm_cols=2 * width)
    f_logit = _norm_proj(h, gain, _pad_cols(w_in_tail, LANES).astype(BF16)[None], 0, out_dtype=F32)
    bias = jnp.pad(forget_bias, (0, LANES - heads)).reshape(1, LANES)
    cum_t = _fox_cum_log_forget(f_logit, bias)[:heads].reshape(heads, 1, -1)
    o = _fox_attention(qkvg, cum_t, heads)
    return _out_proj(o, w_out, slot, h)


def _mixer_c(h, gain, w_in, w_in_tail, w_out, slot, conv_w, a_log, dt_bias, out_norm):
    d = h.shape[1]
    qk_heads = d // HEAD
    v_heads = 2 * qk_heads
    qk_w = qk_heads * HEAD
    v_w = v_heads * HEAD
    main = 2 * qk_w + 2 * v_w
    proj = _gdn_norm_proj_conv(h, gain, w_in, slot, conv_w, n_cols=main, qk_cols=2 * qk_w,
                               q_cols=qk_w)
    ba = _norm_proj(h, gain, _pad_cols(w_in_tail, LANES).astype(BF16)[None], 0, out_dtype=F32)
    gates = _gdn_gates(ba, a_log, dt_bias, v_heads)
    o = _gdn_delta(proj, proj, gates, out_norm, qk_heads, v_heads, z_col0=2 * qk_w + v_w)
    return _out_proj(o, w_out, slot, h)


def kernel(x, mix_norm, ffn_norm, final_norm, a_w_in, a_w_out, a_lam_q1, a_lam_k1, a_lam_q2, a_lam_k2,
           a_sub_norm, b_w_in, b_w_out, b_forget_bias, b_q_norm, b_k_norm, c_w_in, c_w_out, c_conv_w,
           c_a_log, c_dt_bias, c_out_norm, ffn_w_gate, ffn_w_up, ffn_w_down):
    batch, seq, d = x.shape
    depth = mix_norm.shape[0]
    a_in, a_out = a_w_in.astype(BF16), a_w_out.astype(BF16)
    b_in, b_out = b_w_in.astype(BF16), b_w_out.astype(BF16)
    c_in, c_out = c_w_in.astype(BF16), c_w_out.astype(BF16)
    w_gate, w_up, w_down = ffn_w_gate.astype(BF16), ffn_w_up.astype(BF16), ffn_w_down.astype(BF16)
    b_main = 4 * d
    c_main = 6 * d
    outs = []
    for b in range(batch):
        h = x.reshape(seq, d) if batch == 1 else x[b]
        for i in range(depth):
            slot = i // N_MIXERS
            if i % N_MIXERS == 0:
                lambda_init = 0.8 - 0.6 * math.exp(-0.3 * i)
                h = _mixer_a(h, mix_norm[i], a_in, a_out, slot, a_lam_q1[slot], a_lam_k1[slot],
                             a_lam_q2[slot], a_lam_k2[slot], a_sub_norm[slot], lambda_init)
            elif i % N_MIXERS == 1:
                h = _mixer_b(h, mix_norm[i], b_in, b_w_in[slot][:, b_main:], b_out, slot, b_forget_bias[slot],
                             b_q_norm[slot], b_k_norm[slot])
            else:
                h = _mixer_c(h, mix_norm[i], c_in, c_w_in[slot][:, c_main:], c_out, slot, c_conv_w[slot],
                             c_a_log[slot], c_dt_bias[slot], c_out_norm[slot])
            h = _ffn(h, ffn_norm[i], w_gate, w_up, w_down, i,
                     final_gain=final_norm if i == depth - 1 else None)
        outs.append(h)
    return outs[0].reshape(1, seq, d) if batch == 1 else jnp.stack(outs)
```
